```python
import jax, jax.numpy as jnp
from jax import lax
import numpy as np

D_MODEL = 1024
BATCH = 16
SEQ = 2048
DEPTH = 1

RWKV_HEADS = 8
RWKV_HEAD_DIM = 64
RWKV_DIM = RWKV_HEADS * RWKV_HEAD_DIM
DECAY_LORA = 64
AAA_LORA = 64
GATE_LORA = 128
GN_EPS = 64e-5
RWKV_COLS = 3 * RWKV_DIM + DECAY_LORA + AAA_LORA + GATE_LORA
RWKV_SPLITS = (RWKV_DIM, 2 * RWKV_DIM, 3 * RWKV_DIM, 3 * RWKV_DIM + DECAY_LORA,
               3 * RWKV_DIM + DECAY_LORA + AAA_LORA)

ATTN_HEADS = 8
ATTN_KV_HEADS = 2
ATTN_HEAD_DIM = 64
ATTN_GROUP = ATTN_HEADS // ATTN_KV_HEADS
ATTN_DIM = ATTN_HEADS * ATTN_HEAD_DIM
KV_DIM = ATTN_KV_HEADS * ATTN_HEAD_DIM
ATTN_COLS = ATTN_DIM + 2 * KV_DIM
ATTN_SPLITS = (ATTN_DIM, ATTN_DIM + KV_DIM)
WINDOW = 128
BLOCK = 128

IN_COLS = RWKV_COLS + ATTN_COLS + 2 * D_MODEL
IN_SPLITS = (RWKV_COLS, RWKV_COLS + ATTN_COLS, RWKV_COLS + ATTN_COLS + D_MODEL)

D_FF = 2816
CONV_WIDTH = 3
RMS_EPS = 1e-6

kernel_name = "hybrid_rwkv7_swa_sink_alibi_convffn_adaln"


def rms_norm(x, gain):
    xf = x.astype(jnp.float32)
    y = xf * lax.rsqrt(jnp.mean(xf * xf, axis=-1, keepdims=True) + RMS_EPS)
    return (y * gain.astype(jnp.float32)).astype(x.dtype)


def token_shift(t):
    return jnp.pad(t, ((0, 0), (1, 0), (0, 0)))[:, :-1]


def rwkv7_scan(r, decay, k, v, a, b):
    bsz, _, heads, n = r.shape

    def step(state, inp):
        r_t, w_t, k_t, v_t, a_t, b_t = inp
        sa = jnp.einsum('bhvk,bhk->bhv', state, a_t)
        state = (state * w_t[:, :, None, :] + sa[..., None] * b_t[:, :, None, :]
                 + v_t[..., None] * k_t[:, :, None, :])
        y = jnp.einsum('bhvk,bhk->bhv', state, r_t)
        return state, y

    xs = tuple(jnp.swapaxes(t, 0, 1).astype(jnp.float32) for t in (r, decay, k, v, a, b))
    s0 = jnp.zeros((bsz, heads, n, n), jnp.float32)
    _, ys = lax.scan(step, s0, xs)
    return jnp.swapaxes(ys, 0, 1).astype(r.dtype)


def rwkv7_branch(feat, mu, w0, w_up, a0, a_up, g_up, k_k, k_a, r_k, gn_w, gn_b):
    bsz, seq, _ = feat.shape
    hs = (bsz, seq, RWKV_HEADS, RWKV_HEAD_DIM)
    f = feat + (token_shift(feat) - feat) * mu
    r, k, v, w_lo, a_lo, g_lo = jnp.split(f, RWKV_SPLITS, axis=-1)
    w = -jax.nn.softplus(-(w0 + jnp.tanh(w_lo) @ w_up)) - 0.5
    decay = jnp.exp(-jnp.exp(w.astype(jnp.float32)))
    a = jax.nn.sigmoid(a0 + a_lo @ a_up)
    g = jax.nn.sigmoid(g_lo) @ g_up
    kk = (k * k_k).reshape(hs).astype(jnp.float32)
    kk = (kk / jnp.maximum(jnp.sqrt(jnp.sum(kk * kk, -1, keepdims=True)), 1e-12)).astype(k.dtype)
    k = k * (1.0 + (a - 1.0) * k_a)
    a_h = a.reshape(hs)
    r_h, k_h, v_h = r.reshape(hs), k.reshape(hs), v.reshape(hs)
    y = rwkv7_scan(r_h, decay.reshape(hs), k_h, v_h, -kk, kk * a_h)
    yf = y.astype(jnp.float32)
    mean = jnp.mean(yf, -1, keepdims=True)
    var = jnp.mean(jnp.square(yf - mean), -1, keepdims=True)
    y = ((yf - mean) * lax.rsqrt(var + GN_EPS)).astype(feat.dtype)
    y = y * gn_w.reshape(RWKV_HEADS, RWKV_HEAD_DIM) + gn_b.reshape(RWKV_HEADS, RWKV_HEAD_DIM)
    y = y + jnp.sum(r_h * k_h * r_k, -1, keepdims=True) * v_h
    return y.reshape(bsz, seq, RWKV_DIM) * g


def swa_sink_attention(q, k, v, sinks):
    bsz, seq = q.shape[:2]
    nblk = seq // BLOCK
    qb = q.reshape(bsz, nblk, BLOCK, ATTN_KV_HEADS, ATTN_GROUP, ATTN_HEAD_DIM).transpose(1, 0, 2, 3, 4, 5)

    def band(t):
        tp = jnp.pad(t, ((0, 0), (BLOCK, 0), (0, 0), (0, 0)))
        tp = tp.reshape(bsz, nblk + 1, BLOCK, ATTN_KV_HEADS, ATTN_HEAD_DIM)
        return jnp.concatenate([tp[:, :-1], tp[:, 1:]], axis=2).transpose(1, 0, 2, 3, 4)

    kb, vb = band(k), band(v)
    qi = jnp.arange(BLOCK)[:, None]
    sj = jnp.arange(2 * BLOCK)[None, :]
    dist = qi + BLOCK - sj
    in_window = (dist >= 0) & (dist < WINDOW)
    slopes = 2.0 ** (-8.0 * jnp.arange(1, ATTN_HEADS + 1, dtype=jnp.float32) / ATTN_HEADS)
    alibi = -slopes.reshape(ATTN_KV_HEADS, ATTN_GROUP)[:, :, None, None] * dist.astype(jnp.float32)
    sink = sinks.astype(jnp.float32).reshape(ATTN_KV_HEADS, ATTN_GROUP)[None, :, :, None, None]
    scale = ATTN_HEAD_DIM ** -0.5

    def one_block(args):
        blk, q_blk, k_blk, v_blk = args
        s = jnp.einsum('bqkgd,bskd->bkgqs', q_blk, k_blk).astype(jnp.float32) * scale + alibi
        valid = in_window & (blk * BLOCK + sj - BLOCK >= 0)
        s = jnp.where(valid, s, -jnp.inf)
        m = jnp.maximum(jnp.max(s, -1, keepdims=True), sink)
        p = jnp.exp(s - m)
        p = p / (jnp.sum(p, -1, keepdims=True) + jnp.exp(sink - m))
        return jnp.einsum('bkgqs,bskd->bqkgd', p.astype(v_blk.dtype), v_blk)

    out = lax.map(one_block, (jnp.arange(nblk), qb, kb, vb))
    return out.transpose(1, 0, 2, 3, 4, 5).reshape(bsz, seq, ATTN_DIM)


def conv_gated_ffn(u, w_up, conv_w, conv_b, w_down):
    gate, val = jnp.split(u @ w_up, 2, axis=-1)
    gate = lax.conv_general_dilated(gate, conv_w[:, None, :], window_strides=(1,),
                                    padding=[(CONV_WIDTH - 1, 0)],
                                    dimension_numbers=('NWC', 'WIO', 'NWC'),
                                    feature_group_count=D_FF) + conv_b
    return (jax.nn.silu(gate) * val) @ w_down


def setup_inputs(seed: int = 0) -> dict:
    key = jax.random.key(seed)
    ks = jax.random.split(key, 32)
    f32 = jnp.float32
    L = DEPTH

    def nrm(k, shape, s):
        return jax.random.normal(k, shape, f32) * s

    return {
        "x": nrm(ks[0], (BATCH, SEQ, D_MODEL), 1.0),
        "c": nrm(ks[1], (BATCH, D_MODEL), 1.0),
        "ada_w": nrm(ks[2], (L, D_MODEL, 6 * D_MODEL), 0.5 * D_MODEL ** -0.5),
        "ada_b": nrm(ks[3], (L, 6 * D_MODEL), 0.01),
        "norm1_g": 1.0 + nrm(ks[4], (L, D_MODEL), 0.01),
        "w_in": nrm(ks[5], (L, D_MODEL, IN_COLS), D_MODEL ** -0.5),
        "rwkv_mu": jax.random.uniform(ks[6], (L, RWKV_COLS), f32),
        "rwkv_w0": jax.random.uniform(ks[7], (L, RWKV_DIM), f32, -4.0, 0.0),
        "rwkv_w_up": nrm(ks[8], (L, DECAY_LORA, RWKV_DIM), 0.5 * DECAY_LORA ** -0.5),
        "rwkv_a0": nrm(ks[9], (L, RWKV_DIM), 0.1),
        "rwkv_a_up": nrm(ks[10], (L, AAA_LORA, RWKV_DIM), 0.5 * AAA_LORA ** -0.5),
        "rwkv_g_up": nrm(ks[11], (L, GATE_LORA, RWKV_DIM), GATE_LORA ** -0.5),
        "rwkv_k_k": 0.85 + nrm(ks[12], (L, RWKV_DIM), 0.02),
        "rwkv_k_a": 1.0 + nrm(ks[13], (L, RWKV_DIM), 0.02),
        "rwkv_r_k": nrm(ks[14], (L, RWKV_HEADS, RWKV_HEAD_DIM), 0.1),
        "rwkv_gn_w": 1.0 + nrm(ks[15], (L, RWKV_DIM), 0.01),
        "rwkv_gn_b": nrm(ks[16], (L, RWKV_DIM), 0.01),
        "attn_sinks": nrm(ks[17], (L, ATTN_HEADS), 1.0),
        "w_branch_a": nrm(ks[18], (L, RWKV_DIM, D_MODEL), RWKV_DIM ** -0.5),
        "w_branch_b": nrm(ks[19], (L, ATTN_DIM, D_MODEL), ATTN_DIM ** -0.5),
        "w_out": nrm(ks[20], (L, D_MODEL, D_MODEL), D_MODEL ** -0.5),
        "norm2_g": 1.0 + nrm(ks[21], (L, D_MODEL), 0.01),
        "ffn_w_up": nrm(ks[22], (L, D_MODEL, 2 * D_FF), D_MODEL ** -0.5),
        "ffn_conv_w": nrm(ks[23], (L, CONV_WIDTH, D_FF), CONV_WIDTH ** -0.5),
        "ffn_conv_b": nrm(ks[24], (L, D_FF), 0.01),
        "ffn_w_down": nrm(ks[25], (L, D_FF, D_MODEL), D_FF ** -0.5),
        "final_g": 1.0 + nrm(ks[26], (D_MODEL,), 0.01),
    }


def reference(x, c, ada_w, ada_b, norm1_g, w_in, rwkv_mu, rwkv_w0, rwkv_w_up, rwkv_a0, rwkv_a_up,
              rwkv_g_up, rwkv_k_k, rwkv_k_a, rwkv_r_k, rwkv_gn_w, rwkv_gn_b, attn_sinks, w_branch_a,
              w_branch_b, w_out, norm2_g, ffn_w_up, ffn_conv_w, ffn_conv_b, ffn_w_down, final_g):
    bsz, seq, _ = x.shape
    h = x
    c_act = jax.nn.silu(c)
    for l in range(DEPTH):
        mod = (c_act @ ada_w[l] + ada_b[l])[:, None, :]
        sh1, sc1, g1, sh2, sc2, g2 = jnp.split(mod, 6, axis=-1)

        u = rms_norm(h, norm1_g[l]) * (1.0 + sc1) + sh1
        proj = u @ w_in[l]
        f_rwkv, f_attn, gate_a, gate_b = jnp.split(proj, IN_SPLITS, axis=-1)

        y_a = rwkv7_branch(f_rwkv, rwkv_mu[l], rwkv_w0[l], rwkv_w_up[l], rwkv_a0[l], rwkv_a_up[l],
                           rwkv_g_up[l], rwkv_k_k[l], rwkv_k_a[l], rwkv_r_k[l], rwkv_gn_w[l],
                           rwkv_gn_b[l])

        q, k, v = jnp.split(f_attn, ATTN_SPLITS, axis=-1)
        y_b = swa_sink_attention(q.reshape(bsz, seq, ATTN_HEADS, ATTN_HEAD_DIM),
                                 k.reshape(bsz, seq, ATTN_KV_HEADS, ATTN_HEAD_DIM),
                                 v.reshape(bsz, seq, ATTN_KV_HEADS, ATTN_HEAD_DIM),
                                 attn_sinks[l])

        merged = (jax.nn.sigmoid(gate_a) * (y_a @ w_branch_a[l])
                  + jax.nn.sigmoid(gate_b) * (y_b @ w_branch_b[l]))
        h = h + g1 * (merged @ w_out[l])

        u2 = rms_norm(h, norm2_g[l]) * (1.0 + sc2) + sh2
        h = h + g2 * conv_gated_ffn(u2, ffn_w_up[l], ffn_conv_w[l], ffn_conv_b[l], ffn_w_down[l])
    return rms_norm(h, final_g)
```

```python
import functools

import jax
import jax.numpy as jnp
from jax.experimental import pallas as pl
from jax.experimental.pallas import tpu as pltpu

F32 = jnp.float32
BF16 = jnp.bfloat16

D_MODEL = 1024
RWKV_HEADS = 8
HEAD_DIM = 64
RWKV_DIM = RWKV_HEADS * HEAD_DIM
DECAY_LORA = 64
AAA_LORA = 64
GATE_LORA = 128
LORA_COLS = DECAY_LORA + AAA_LORA
RWKV_COLS = 3 * RWKV_DIM + LORA_COLS + GATE_LORA
ATTN_HEADS = 8
ATTN_KV_HEADS = 2
ATTN_GROUP = ATTN_HEADS // ATTN_KV_HEADS
ATTN_DIM = ATTN_HEADS * HEAD_DIM
KV_DIM = ATTN_KV_HEADS * HEAD_DIM
ATTN_BLOCK = 128
D_FF = 2816
GN_EPS = 64e-5
RMS_EPS = 1e-6
DECAY_SCALE = 0.6065306597126334

VMEM_LIMIT_BYTES = 56 * 1024 * 1024
ROW_TILE = 512
FFN_CHUNK = 256
SCAN_STEPS = 16
SUBLANES = 8
NEG_BIG = -1e30


def _params(*semantics):
    return pltpu.CompilerParams(dimension_semantics=semantics, vmem_limit_bytes=VMEM_LIMIT_BYTES)


def _resident(shape):
    zeros = (0,) * len(shape)
    return pl.BlockSpec(shape, lambda *_: zeros, pipeline_mode=pl.Buffered(1))


def _bdot(a, b):
    return jnp.dot(a.astype(BF16), b.astype(BF16), preferred_element_type=F32)


def _split(x):
    hi = x.astype(BF16)
    lo = (x - hi.astype(F32)).astype(BF16)
    return hi, lo


def _dot3(a, b):
    a_hi, a_lo = _split(a)
    b_hi, b_lo = _split(b)
    dot = functools.partial(jnp.dot, preferred_element_type=F32)
    return dot(a_hi, b_hi) + dot(a_lo, b_hi) + dot(a_hi, b_lo)


def _head_sum(x, ones_blockdiag):
    hi, lo = _split(x)
    dot = functools.partial(jnp.dot, preferred_element_type=F32)
    return dot(hi, ones_blockdiag) + dot(lo, ones_blockdiag)


def _rms_mod(h, gain, scale, shift):
    y = h * jax.lax.rsqrt(jnp.mean(h * h, axis=-1, keepdims=True) + RMS_EPS)
    return y * gain * (1.0 + scale) + shift


def _mod_kernel(c_ref, w_ref, b_ref, o_ref):
    c = c_ref[...]
    o_ref[...] = _bdot(c * jax.nn.sigmoid(c), w_ref[...]) + b_ref[...]


def _mod(c, ada_w, ada_b):
    bsz = c.shape[0]
    cols = ada_w.shape[1]
    tile = cols // 4
    return pl.pallas_call(
        _mod_kernel,
        grid=(cols // tile,),
        in_specs=[pl.BlockSpec((bsz, D_MODEL), lambda j: (0, 0)),
                  pl.BlockSpec((D_MODEL, tile), lambda j: (0, j)),
                  pl.BlockSpec((1, tile), lambda j: (0, j))],
        out_specs=pl.BlockSpec((bsz, tile), lambda j: (0, j)),
        out_shape=jax.ShapeDtypeStruct((bsz, cols), F32),
        compiler_params=_params("arbitrary"),
        name="mod",
    )(c, ada_w, ada_b.reshape(1, cols))


def _in_proj_kernel(x_ref, mod_ref, g_ref, w_ref, f_ref, q_ref, kv_ref, ga_ref, gb_ref):
    u = _rms_mod(x_ref[...], g_ref[...], mod_ref[0, 1:2, :], mod_ref[0, 0:1, :]).astype(BF16)
    col = 0
    for out in (f_ref, q_ref, kv_ref, ga_ref, gb_ref):
        width = out.shape[1]
        out[...] = jnp.dot(u, w_ref[:, col:col + width], preferred_element_type=F32).astype(out.dtype)
        col += width


def _in_proj(x2, mod3, norm_g, w_in, tiles_per_seq):
    rows = x2.shape[0]
    widths = (RWKV_COLS, ATTN_DIM, 2 * KV_DIM, D_MODEL, D_MODEL)
    dtypes = (F32, BF16, BF16, BF16, BF16)
    return pl.pallas_call(
        _in_proj_kernel,
        grid=(rows // ROW_TILE,),
        in_specs=[pl.BlockSpec((ROW_TILE, D_MODEL), lambda i: (i, 0)),
                  pl.BlockSpec((1, 6, D_MODEL), lambda i: (i // tiles_per_seq, 0, 0)),
                  _resident((1, D_MODEL)),
                  _resident(w_in.shape)],
        out_specs=[pl.BlockSpec((ROW_TILE, w), lambda i: (i, 0)) for w in widths],
        out_shape=[jax.ShapeDtypeStruct((rows, w), d) for w, d in zip(widths, dtypes)],
        compiler_params=_params("arbitrary"),
        name="in_proj",
    )(x2, mod3, norm_g.reshape(1, D_MODEL), w_in)


def _prep_kernel(tiles_per_seq, f_ref, prev_ref, mu_ref, lora_ref, gup_ref, vec_ref, ones_ref,
                 r_ref, w_ref, k_ref, v_ref, kk_ref, b_ref, g_ref, bonus_ref):
    feat = f_ref[...]
    seq_start = pl.program_id(0) % tiles_per_seq == 0
    prev_row = jnp.where(seq_start, 0.0, prev_ref[SUBLANES - 1:SUBLANES, :])
    row = jax.lax.broadcasted_iota(jnp.int32, feat.shape, 0)
    shifted = jnp.where(row == 0, prev_row, pltpu.roll(feat, 1, 0))
    f = feat + (shifted - feat) * mu_ref[...]

    r = f[:, 0:RWKV_DIM]
    k = f[:, RWKV_DIM:2 * RWKV_DIM]
    v = f[:, 2 * RWKV_DIM:3 * RWKV_DIM]
    lo = f[:, 3 * RWKV_DIM:3 * RWKV_DIM + LORA_COLS]
    g_lo = f[:, 3 * RWKV_DIM + LORA_COLS:]
    w0, a0, k_k, k_a, r_k = (vec_ref[i:i + 1, :] for i in range(5))

    lane = jax.lax.broadcasted_iota(jnp.int32, lo.shape, 1)
    lora = _dot3(jnp.where(lane < DECAY_LORA, jnp.tanh(lo), lo), lora_ref[...])
    decay = jnp.exp(-DECAY_SCALE * jax.nn.sigmoid(w0 + lora[:, :RWKV_DIM]))
    a = jax.nn.sigmoid(a0 + lora[:, RWKV_DIM:])
    g = _bdot(jax.nn.sigmoid(g_lo), gup_ref[...])

    ones = ones_ref[...]
    kk = k * k_k
    kk = kk / jnp.maximum(jnp.sqrt(_head_sum(kk * kk, ones)), 1e-12)
    k = k * (1.0 + (a - 1.0) * k_a)

    r_ref[...] = r
    w_ref[...] = decay
    k_ref[...] = k
    v_ref[...] = v
    kk_ref[...] = kk
    b_ref[...] = kk * a
    g_ref[...] = g
    bonus_ref[...] = _head_sum(r * k * r_k, ones) * v


def _prep(f_rwkv, mu, lora_w, g_up, vecs, ones_bd, tiles_per_seq):
    rows = f_rwkv.shape[0]
    prev_blocks = ROW_TILE // SUBLANES
    out = pl.BlockSpec((ROW_TILE, RWKV_DIM), lambda i: (i, 0))
    return pl.pallas_call(
        functools.partial(_prep_kernel, tiles_per_seq),
        grid=(rows // ROW_TILE,),
        in_specs=[pl.BlockSpec((ROW_TILE, RWKV_COLS), lambda i: (i, 0)),
                  pl.BlockSpec((SUBLANES, RWKV_COLS), lambda i: (jnp.maximum(i * prev_blocks - 1, 0), 0)),
                  _resident(mu.shape), _resident(lora_w.shape), _resident(g_up.shape),
                  _resident(vecs.shape), _resident(ones_bd.shape)],
        out_specs=[out] * 8,
        out_shape=[jax.ShapeDtypeStruct((rows, RWKV_DIM), F32)] * 8,
        compiler_params=_params("arbitrary"),
        name="rwkv_prep",
    )(f_rwkv, f_rwkv, mu, lora_w, g_up, vecs, ones_bd)


def _scan_kernel(r_ref, w_ref, k_ref, v_ref, kk_ref, b_ref, y_ref, state_ref):
    @pl.when(pl.program_id(0) == 0)
    def _():
        state_ref[...] = jnp.zeros_like(state_ref)

    def step(t, carry):
        acc = jnp.zeros((HEAD_DIM, 128), F32)
        for j in range(HEAD_DIM):
            acc = acc + state_ref[j] * kk_ref[t, j:j + 1, :]
        sa = -acc
        v = v_ref[t]
        y = jnp.zeros((HEAD_DIM, 128), F32)
        for j in range(HEAD_DIM):
            s_new = (state_ref[j] * w_ref[t, j:j + 1, :] + sa * b_ref[t, j:j + 1, :]
                     + v * k_ref[t, j:j + 1, :])
            state_ref[j] = s_new
            y = y + s_new * r_ref[t, j:j + 1, :]
        mean = jnp.mean(y, axis=0, keepdims=True)
        cen = y - mean
        var = jnp.mean(cen * cen, axis=0, keepdims=True)
        y_ref[t] = cen * jax.lax.rsqrt(var + GN_EPS)
        return carry

    jax.lax.fori_loop(0, SCAN_STEPS, step, 0)


def _scan(r, w, k, v, kk, b):
    seq = r.shape[0]
    blk = pl.BlockSpec((SCAN_STEPS, HEAD_DIM, 128), lambda i: (i, 0, 0))
    return pl.pallas_call(
        _scan_kernel,
        grid=(seq // SCAN_STEPS,),
        in_specs=[blk] * 6,
        out_specs=blk,
        out_shape=jax.ShapeDtypeStruct(r.shape, F32),
        scratch_shapes=[pltpu.VMEM((HEAD_DIM, HEAD_DIM, 128), F32)],
        compiler_params=_params("arbitrary"),
        name="rwkv_scan",
    )(r, w, k, v, kk, b)


def _attn_kernel(q_ref, kv_ref, kvp_ref, sink_ref, o_ref):
    first_key = jnp.where(pl.program_id(1) == 0, ATTN_BLOCK, 0)
    qi = jax.lax.broadcasted_iota(jnp.int32, (ATTN_BLOCK, 2 * ATTN_BLOCK), 0)
    sj = jax.lax.broadcasted_iota(jnp.int32, (ATTN_BLOCK, 2 * ATTN_BLOCK), 1)
    dist = qi + ATTN_BLOCK - sj
    valid = (dist >= 0) & (dist < ATTN_BLOCK) & (sj >= first_key)
    distf = dist.astype(F32)
    scale = HEAD_DIM ** -0.5
    kv = jnp.concatenate([kvp_ref[...], kv_ref[...]], axis=0)
    q = q_ref[...]
    for h in range(ATTN_HEADS):
        kvh = h // ATTN_GROUP
        k_h = kv[:, kvh * HEAD_DIM:(kvh + 1) * HEAD_DIM]
        v_h = kv[:, KV_DIM + kvh * HEAD_DIM:KV_DIM + (kvh + 1) * HEAD_DIM]
        s = jax.lax.dot_general(q[:, h * HEAD_DIM:(h + 1) * HEAD_DIM], k_h,
                                (((1,), (1,)), ((), ())), preferred_element_type=F32)
        slope = 2.0 ** (-8.0 * (h + 1) / ATTN_HEADS)
        s = jnp.where(valid, s * scale - slope * distf, NEG_BIG)
        sink = sink_ref[h]
        m = jnp.maximum(jnp.max(s, axis=-1, keepdims=True), sink)
        p = jnp.exp(s - m)
        denom = jnp.sum(p, axis=-1, keepdims=True) + jnp.exp(sink - m)
        o = jnp.dot(p.astype(BF16), v_h, preferred_element_type=F32) / denom
        o_ref[:, h * HEAD_DIM:(h + 1) * HEAD_DIM] = o.astype(o_ref.dtype)


def _attn(q, kv, sinks, blocks_per_seq):
    rows = q.shape[0]
    return pl.pallas_call(
        _attn_kernel,
        grid=(rows // ATTN_BLOCK // blocks_per_seq, blocks_per_seq),
        in_specs=[pl.BlockSpec((ATTN_BLOCK, ATTN_DIM), lambda b, i: (b * blocks_per_seq + i, 0)),
                  pl.BlockSpec((ATTN_BLOCK, 2 * KV_DIM), lambda b, i: (b * blocks_per_seq + i, 0)),
                  pl.BlockSpec((ATTN_BLOCK, 2 * KV_DIM),
                               lambda b, i: (b * blocks_per_seq + jnp.maximum(i - 1, 0), 0)),
                  pl.BlockSpec(memory_space=pltpu.SMEM)],
        out_specs=pl.BlockSpec((ATTN_BLOCK, ATTN_DIM), lambda b, i: (b * blocks_per_seq + i, 0)),
        out_shape=jax.ShapeDtypeStruct((rows, ATTN_DIM), BF16),
        compiler_params=_params("arbitrary", "arbitrary"),
        name="swa_attn",
    )(q, kv, kv, sinks)


def _merge_kernel(x_ref, mod_ref, yn_ref, bonus_ref, g_ref, yb_ref, ga_ref, gb_ref, gn_ref, n2_ref,
                  wa_ref, wb_ref, wo_ref, h_ref, u_ref):
    ya = ((yn_ref[...] * gn_ref[0:1, :] + gn_ref[1:2, :]) + bonus_ref[...]) * g_ref[...]
    pa = _bdot(ya, wa_ref[...])
    pb = jnp.dot(yb_ref[...], wb_ref[...], preferred_element_type=F32)
    merged = (jax.nn.sigmoid(ga_ref[...].astype(F32)) * pa
              + jax.nn.sigmoid(gb_ref[...].astype(F32)) * pb)
    h = x_ref[...] + mod_ref[0, 2:3, :] * _bdot(merged, wo_ref[...])
    h_ref[...] = h
    u_ref[...] = _rms_mod(h, n2_ref[...], mod_ref[0, 4:5, :], mod_ref[0, 3:4, :]).astype(BF16)


def _merge(x2, mod3, yn, bonus, g, yb, ga, gb, gn, norm2_g, wa, wb, wo, tiles_per_seq):
    rows = x2.shape[0]

    def tile(width):
        return pl.BlockSpec((ROW_TILE, width), lambda i: (i, 0))

    return pl.pallas_call(
        _merge_kernel,
        grid=(rows // ROW_TILE,),
        in_specs=[tile(D_MODEL),
                  pl.BlockSpec((1, 6, D_MODEL), lambda i: (i // tiles_per_seq, 0, 0)),
                  tile(RWKV_DIM), tile(RWKV_DIM), tile(RWKV_DIM), tile(ATTN_DIM),
                  tile(D_MODEL), tile(D_MODEL),
                  _resident(gn.shape), _resident((1, D_MODEL)),
                  _resident(wa.shape), _resident(wb.shape), _resident(wo.shape)],
        out_specs=[tile(D_MODEL), tile(D_MODEL)],
        out_shape=[jax.ShapeDtypeStruct((rows, D_MODEL), F32),
                   jax.ShapeDtypeStruct((rows, D_MODEL), BF16)],
        compiler_params=_params("arbitrary"),
        name="merge",
    )(x2, mod3, yn, bonus, g, yb, ga, gb, gn, norm2_g.reshape(1, D_MODEL), wa, wb, wo)


def _ffn_kernel(tiles_per_seq, h_ref, u_ref, mod_ref, wg_ref, wv_ref, cw_ref, cb_ref, wd_ref, fg_ref,
                o_ref, carry_ref):
    @pl.when(pl.program_id(0) % tiles_per_seq == 0)
    def _():
        carry_ref[...] = jnp.zeros_like(carry_ref)

    u = u_ref[...]
    row = jax.lax.broadcasted_iota(jnp.int32, (ROW_TILE, FFN_CHUNK), 0)
    acc = jnp.zeros((ROW_TILE, D_MODEL), F32)
    for c in range(D_FF // FFN_CHUNK):
        cols = slice(c * FFN_CHUNK, (c + 1) * FFN_CHUNK)
        gate = jnp.dot(u, wg_ref[:, cols], preferred_element_type=F32)
        val = jnp.dot(u, wv_ref[:, cols], preferred_element_type=F32)
        back1 = carry_ref[SUBLANES - 1:SUBLANES, cols]
        back2 = carry_ref[SUBLANES - 2:SUBLANES - 1, cols]
        carry_ref[:, cols] = gate[ROW_TILE - SUBLANES:, :]
        shift1 = jnp.where(row == 0, back1, pltpu.roll(gate, 1, 0))
        shift2 = jnp.where(row == 0, back2, jnp.where(row == 1, back1, pltpu.roll(gate, 2, 0)))
        conv = (cw_ref[0:1, cols] * shift2 + cw_ref[1:2, cols] * shift1 + cw_ref[2:3, cols] * gate
                + cb_ref[:, cols])
        act = conv * jax.nn.sigmoid(conv) * val
        acc = acc + jnp.dot(act.astype(BF16), wd_ref[cols, :], preferred_element_type=F32)
    h = h_ref[...] + mod_ref[0, 5:6, :] * acc
    o_ref[...] = h * jax.lax.rsqrt(jnp.mean(h * h, axis=-1, keepdims=True) + RMS_EPS) * fg_ref[...]


def _ffn(h1, u2, mod3, w_gate, w_val, conv_w, conv_b, w_down, final_g, tiles_per_seq):
    rows = h1.shape[0]
    return pl.pallas_call(
        functools.partial(_ffn_kernel, tiles_per_seq),
        grid=(rows // ROW_TILE,),
        in_specs=[pl.BlockSpec((ROW_TILE, D_MODEL), lambda i: (i, 0)),
                  pl.BlockSpec((ROW_TILE, D_MODEL), lambda i: (i, 0)),
                  pl.BlockSpec((1, 6, D_MODEL), lambda i: (i // tiles_per_seq, 0, 0)),
                  _resident(w_gate.shape), _resident(w_val.shape), _resident(conv_w.shape),
                  _resident((1, D_FF)), _resident(w_down.shape), _resident((1, D_MODEL))],
        out_specs=pl.BlockSpec((ROW_TILE, D_MODEL), lambda i: (i, 0)),
        out_shape=jax.ShapeDtypeStruct((rows, D_MODEL), F32),
        scratch_shapes=[pltpu.VMEM((SUBLANES, D_FF), F32)],
        compiler_params=_params("arbitrary"),
        name="ffn",
    )(h1, u2, mod3, w_gate, w_val, conv_w, conv_b.reshape(1, D_FF), w_down, final_g.reshape(1, D_MODEL))


def _to_lanes(t, bsz, seq):
    return t.reshape(bsz, seq, RWKV_HEADS, HEAD_DIM).transpose(1, 3, 0, 2).reshape(seq, HEAD_DIM, bsz * RWKV_HEADS)


def _from_lanes(t, bsz, seq):
    return t.reshape(seq, HEAD_DIM, bsz, RWKV_HEADS).transpose(2, 0, 3, 1).reshape(bsz * seq, RWKV_DIM)


def kernel(x, c, ada_w, ada_b, norm1_g, w_in, rwkv_mu, rwkv_w0, rwkv_w_up, rwkv_a0, rwkv_a_up, rwkv_g_up, rwkv_k_k, rwkv_k_a, rwkv_r_k, rwkv_gn_w, rwkv_gn_b, attn_sinks, w_branch_a, w_branch_b, w_out, norm2_g, ffn_w_up, ffn_conv_w, ffn_conv_b, ffn_w_down, final_g):
    bsz, seq, _ = x.shape
    depth = ada_w.shape[0]
    assert depth == 1, "the ffn kernel fuses the final norm, which follows the last layer only"
    assert bsz * RWKV_HEADS == 128 and seq % ROW_TILE == 0 and seq % SCAN_STEPS == 0
    tiles_per_seq = seq // ROW_TILE
    ones_bd = jnp.kron(jnp.eye(RWKV_HEADS, dtype=F32), jnp.ones((HEAD_DIM, HEAD_DIM), F32)).astype(BF16)
    h = x.reshape(bsz * seq, D_MODEL)
    for l in range(depth):
        mod3 = _mod(c, ada_w[l], ada_b[l]).reshape(bsz, 6, D_MODEL)
        f_rwkv, q, kv, gate_a, gate_b = _in_proj(h, mod3, norm1_g[l], w_in[l].astype(BF16), tiles_per_seq)

        lora_w = jnp.zeros((LORA_COLS, 2 * RWKV_DIM), F32)
        lora_w = lora_w.at[:DECAY_LORA, :RWKV_DIM].set(rwkv_w_up[l]).at[DECAY_LORA:, RWKV_DIM:].set(rwkv_a_up[l])
        vecs = jnp.stack([rwkv_w0[l], rwkv_a0[l], rwkv_k_k[l], rwkv_k_a[l], rwkv_r_k[l].reshape(RWKV_DIM)])
        r, w, k, v, kk, b, g, bonus = _prep(f_rwkv, rwkv_mu[l].reshape(1, RWKV_COLS), lora_w, rwkv_g_up[l],
                                            vecs, ones_bd, tiles_per_seq)
        yn = _scan(*(_to_lanes(t, bsz, seq) for t in (r, w, k, v, kk, b)))
        yn = _from_lanes(yn, bsz, seq)

        yb = _attn(q, kv, attn_sinks[l], seq // ATTN_BLOCK)

        gn = jnp.stack([rwkv_gn_w[l], rwkv_gn_b[l]])
        h, u2 = _merge(h, mod3, yn, bonus, g, yb, gate_a, gate_b, gn, norm2_g[l],
                       w_branch_a[l].astype(BF16), w_branch_b[l].astype(BF16), w_out[l].astype(BF16),
                       tiles_per_seq)
        w_up = ffn_w_up[l].astype(BF16)
        h = _ffn(h, u2, mod3, w_up[:, :D_FF], w_up[:, D_FF:], ffn_conv_w[l], ffn_conv_b[l],
                 ffn_w_down[l].astype(BF16), final_g, tiles_per_seq)
    return h.reshape(bsz, seq, D_MODEL)
```

```python
import functools

import jax
import jax.numpy as jnp
from jax.experimental import pallas as pl
from jax.experimental.pallas import tpu as pltpu

F32 = jnp.float32
BF16 = jnp.bfloat16

D_MODEL = 1024
RWKV_HEADS = 8
HEAD_DIM = 64
RWKV_DIM = RWKV_HEADS * HEAD_DIM
DECAY_LORA = 64
AAA_LORA = 64
GATE_LORA = 128
LORA_COLS = DECAY_LORA + AAA_LORA
RWKV_COLS = 3 * RWKV_DIM + LORA_COLS + GATE_LORA
ATTN_HEADS = 8
ATTN_KV_HEADS = 2
ATTN_GROUP = ATTN_HEADS // ATTN_KV_HEADS
ATTN_DIM = ATTN_HEADS * HEAD_DIM
KV_DIM = ATTN_KV_HEADS * HEAD_DIM
ATTN_BLOCK = 128
D_FF = 2816
GN_EPS = 64e-5
RMS_EPS = 1e-6
DECAY_SCALE = 0.6065306597126334

VMEM_LIMIT_BYTES = 56 * 1024 * 1024
ROW_TILE = 512
FFN_CHUNK = 256
FFN_LOOKAHEAD = 2
SCAN_STEPS = 32
SCAN_INPUTS = 6
UPDATE_GROUPS = 2
SUBLANES = 8
NEG_BIG = -1e30


def _params(*semantics):
    return pltpu.CompilerParams(dimension_semantics=semantics, vmem_limit_bytes=VMEM_LIMIT_BYTES)


def _resident(shape):
    zeros = (0,) * len(shape)
    return pl.BlockSpec(shape, lambda *_: zeros, pipeline_mode=pl.Buffered(1))


def _bdot(a, b):
    return jnp.dot(a.astype(BF16), b.astype(BF16), preferred_element_type=F32)


def _split(x):
    hi = x.astype(BF16)
    lo = (x - hi.astype(F32)).astype(BF16)
    return hi, lo


def _dot3(a, b):
    a_hi, a_lo = _split(a)
    b_hi, b_lo = _split(b)
    dot = functools.partial(jnp.dot, preferred_element_type=F32)
    return dot(a_hi, b_hi) + dot(a_lo, b_hi) + dot(a_hi, b_lo)


def _head_sum(x, ones_blockdiag):
    hi, lo = _split(x)
    dot = functools.partial(jnp.dot, preferred_element_type=F32)
    return dot(hi, ones_blockdiag) + dot(lo, ones_blockdiag)


def _rms_mod(h, gain, scale, shift):
    y = h * jax.lax.rsqrt(jnp.mean(h * h, axis=-1, keepdims=True) + RMS_EPS)
    return y * gain * (1.0 + scale) + shift


def _mod_kernel(c_ref, w_ref, b_ref, o_ref):
    c = c_ref[...]
    o_ref[...] = _bdot(c * jax.nn.sigmoid(c), w_ref[...]) + b_ref[...]


def _mod(c, ada_w, ada_b):
    bsz = c.shape[0]
    cols = ada_w.shape[1]
    tile = cols // 4
    return pl.pallas_call(
        _mod_kernel,
        grid=(cols // tile,),
        in_specs=[pl.BlockSpec((bsz, D_MODEL), lambda j: (0, 0)),
                  pl.BlockSpec((D_MODEL, tile), lambda j: (0, j)),
                  pl.BlockSpec((1, tile), lambda j: (0, j))],
        out_specs=pl.BlockSpec((bsz, tile), lambda j: (0, j)),
        out_shape=jax.ShapeDtypeStruct((bsz, cols), F32),
        compiler_params=_params("arbitrary"),
        name="mod",
    )(c, ada_w, ada_b.reshape(1, cols))


def _in_proj_kernel(x_ref, mod_ref, g_ref, w_ref, f_ref, q_ref, kv_ref, ga_ref, gb_ref):
    u = _rms_mod(x_ref[...], g_ref[...], mod_ref[0, 1:2, :], mod_ref[0, 0:1, :]).astype(BF16)
    col = 0
    for out in (f_ref, q_ref, kv_ref, ga_ref, gb_ref):
        width = out.shape[1]
        out[...] = jnp.dot(u, w_ref[:, col:col + width], preferred_element_type=F32).astype(out.dtype)
        col += width


def _in_proj(x2, mod3, norm_g, w_in, tiles_per_seq):
    rows = x2.shape[0]
    widths = (RWKV_COLS, ATTN_DIM, 4 * KV_DIM, D_MODEL, D_MODEL)
    dtypes = (F32, BF16, BF16, BF16, BF16)
    return pl.pallas_call(
        _in_proj_kernel,
        grid=(rows // ROW_TILE,),
        in_specs=[pl.BlockSpec((ROW_TILE, D_MODEL), lambda i: (i, 0)),
                  pl.BlockSpec((1, 6, D_MODEL), lambda i: (i // tiles_per_seq, 0, 0)),
                  _resident((1, D_MODEL)),
                  _resident(w_in.shape)],
        out_specs=[pl.BlockSpec((ROW_TILE, w), lambda i: (i, 0)) for w in widths],
        out_shape=[jax.ShapeDtypeStruct((rows, w), d) for w, d in zip(widths, dtypes)],
        compiler_params=_params("arbitrary"),
        name="in_proj",
    )(x2, mod3, norm_g.reshape(1, D_MODEL), w_in)


def _prep_kernel(tiles_per_seq, f_ref, prev_ref, mu_ref, lora_ref, gup_ref, vec_ref, ones_ref,
                 x_ref, g_ref, bonus_ref):
    feat = f_ref[...]
    seq_start = pl.program_id(0) % tiles_per_seq == 0
    prev_row = jnp.where(seq_start, 0.0, prev_ref[SUBLANES - 1:SUBLANES, :])
    row = jax.lax.broadcasted_iota(jnp.int32, feat.shape, 0)
    shifted = jnp.where(row == 0, prev_row, pltpu.roll(feat, 1, 0))
    f = feat + (shifted - feat) * mu_ref[...]

    r = f[:, 0:RWKV_DIM]
    k = f[:, RWKV_DIM:2 * RWKV_DIM]
    v = f[:, 2 * RWKV_DIM:3 * RWKV_DIM]
    lo = f[:, 3 * RWKV_DIM:3 * RWKV_DIM + LORA_COLS]
    g_lo = f[:, 3 * RWKV_DIM + LORA_COLS:]
    w0, a0, k_k, k_a, r_k = (vec_ref[i:i + 1, :] for i in range(5))

    lane = jax.lax.broadcasted_iota(jnp.int32, lo.shape, 1)
    lora = _dot3(jnp.where(lane < DECAY_LORA, jnp.tanh(lo), lo), lora_ref[...])
    decay = jnp.exp(-DECAY_SCALE * jax.nn.sigmoid(w0 + lora[:, :RWKV_DIM]))
    a = jax.nn.sigmoid(a0 + lora[:, RWKV_DIM:])
    g = _bdot(jax.nn.sigmoid(g_lo), gup_ref[...])

    ones = ones_ref[...]
    kk = k * k_k
    kk = kk / jnp.maximum(jnp.sqrt(_head_sum(kk * kk, ones)), 1e-12)
    k = k * (1.0 + (a - 1.0) * k_a)

    for i, t in enumerate((r, decay, k, v, kk, kk * a)):
        x_ref[i] = t
    g_ref[...] = g
    bonus_ref[...] = _head_sum(r * k * r_k, ones) * v


def _prep(f_rwkv, mu, lora_w, g_up, vecs, ones_bd, tiles_per_seq):
    rows = f_rwkv.shape[0]
    prev_blocks = ROW_TILE // SUBLANES
    out = pl.BlockSpec((ROW_TILE, RWKV_DIM), lambda i: (i, 0))
    return pl.pallas_call(
        functools.partial(_prep_kernel, tiles_per_seq),
        grid=(rows // ROW_TILE,),
        in_specs=[pl.BlockSpec((ROW_TILE, RWKV_COLS), lambda i: (i, 0)),
                  pl.BlockSpec((SUBLANES, RWKV_COLS), lambda i: (jnp.maximum(i * prev_blocks - 1, 0), 0)),
                  _resident(mu.shape), _resident(lora_w.shape), _resident(g_up.shape),
                  _resident(vecs.shape), _resident(ones_bd.shape)],
        out_specs=[pl.BlockSpec((SCAN_INPUTS, ROW_TILE, RWKV_DIM), lambda i: (0, i, 0)), out, out],
        out_shape=[jax.ShapeDtypeStruct((SCAN_INPUTS, rows, RWKV_DIM), F32)]
        + [jax.ShapeDtypeStruct((rows, RWKV_DIM), F32)] * 2,
        compiler_params=_params("arbitrary"),
        name="rwkv_prep",
    )(f_rwkv, f_rwkv, mu, lora_w, g_up, vecs, ones_bd)


def _scan_kernel(x_ref, y_ref, state_ref, rows_even_ref, rows_odd_ref, yraw_ref):
    pairs = state_ref.shape[-1]
    R, W, K, V, KK, B = range(SCAN_INPUTS)
    per_group = HEAD_DIM // UPDATE_GROUPS

    def transpose_in(i, t, rows_ref):
        xt = x_ref[i, :, pl.ds(t, 1)].reshape(pairs, HEAD_DIM).T
        rows_ref[i] = xt.reshape(SUBLANES, SUBLANES, pairs)

    def write_out(t):
        y = yraw_ref[...]
        cen = y - jnp.mean(y, axis=0, keepdims=True)
        var = jnp.mean(cen * cen, axis=0, keepdims=True)
        yn = cen * jax.lax.rsqrt(var + GN_EPS)
        y_ref[:, pl.ds(t, 1)] = yn.T.reshape(-1, 1, RWKV_HEADS, HEAD_DIM)

    @pl.when(pl.program_id(0) == 0)
    def _():
        state_ref[...] = jnp.zeros_like(state_ref)
        yraw_ref[...] = jnp.zeros_like(yraw_ref)

    for i in range(SCAN_INPUTS):
        transpose_in(i, 0, rows_even_ref)

    def step(t, rows_ref, next_rows_ref):
        t_next = jnp.minimum(t + 1, SCAN_STEPS - 1)
        write_out(jnp.maximum(t - 1, 0))

        acc = jnp.zeros((HEAD_DIM, pairs), F32)
        for j in range(HEAD_DIM):
            acc = acc + state_ref[j] * rows_ref[KK, j // SUBLANES, j % SUBLANES:j % SUBLANES + 1, :]
        sa = -acc
        v = rows_ref[V].reshape(HEAD_DIM, pairs)

        def update(grp, y):
            for u in range(SCAN_INPUTS // UPDATE_GROUPS):
                transpose_in(grp * (SCAN_INPUTS // UPDATE_GROUPS) + u, t_next, next_rows_ref)
            for jj in range(per_group):
                j = grp * per_group + jj
                jh = grp * (per_group // SUBLANES) + jj // SUBLANES
                kl = slice(jj % SUBLANES, jj % SUBLANES + 1)
                s_new = (state_ref[j] * rows_ref[W, jh, kl, :] + sa * rows_ref[B, jh, kl, :]
                         + v * rows_ref[K, jh, kl, :])
                state_ref[j] = s_new
                y = y + s_new * rows_ref[R, jh, kl, :]
            return y

        yraw_ref[...] = jax.lax.fori_loop(0, UPDATE_GROUPS, update, jnp.zeros((HEAD_DIM, pairs), F32))

    def two_steps(i, carry):
        step(2 * i, rows_even_ref, rows_odd_ref)
        step(2 * i + 1, rows_odd_ref, rows_even_ref)
        return carry

    jax.lax.fori_loop(0, SCAN_STEPS // 2, two_steps, 0)
    write_out(SCAN_STEPS - 1)


def _scan(x):
    _, bsz, seq = x.shape[:3]
    pairs = bsz * RWKV_HEADS
    return pl.pallas_call(
        _scan_kernel,
        grid=(seq // SCAN_STEPS,),
        in_specs=[pl.BlockSpec((SCAN_INPUTS, bsz, SCAN_STEPS, RWKV_HEADS, HEAD_DIM),
                               lambda i: (0, 0, i, 0, 0))],
        out_specs=pl.BlockSpec((bsz, SCAN_STEPS, RWKV_HEADS, HEAD_DIM), lambda i: (0, i, 0, 0)),
        out_shape=jax.ShapeDtypeStruct(x.shape[1:], F32),
        scratch_shapes=[pltpu.VMEM((HEAD_DIM, HEAD_DIM, pairs), F32),
                        pltpu.VMEM((SCAN_INPUTS, SUBLANES, SUBLANES, pairs), F32),
                        pltpu.VMEM((SCAN_INPUTS, SUBLANES, SUBLANES, pairs), F32),
                        pltpu.VMEM((HEAD_DIM, pairs), F32)],
        compiler_params=_params("arbitrary"),
        name="rwkv_scan",
    )(x)


def _attn_kernel(q_ref, kv_ref, kvp_ref, bias_ref, sink_ref, o_ref):
    kv = jnp.concatenate([kvp_ref[...], kv_ref[...]], axis=0)
    low = jax.lax.broadcasted_iota(jnp.int32, (2 * ATTN_BLOCK, 2 * HEAD_DIM), 1) < HEAD_DIM
    low_out = jax.lax.broadcasted_iota(jnp.int32, (ATTN_BLOCK, 2 * HEAD_DIM), 1) < HEAD_DIM
    nothing = jnp.zeros((2 * ATTN_BLOCK, 2 * HEAD_DIM), BF16)

    def block_diag(t):
        return jnp.concatenate([jnp.where(low, t, nothing), jnp.where(low, nothing, t)], axis=0)

    pairs = range(ATTN_HEADS // 2)
    pairs_per_kv = ATTN_GROUP // 2
    lanes = [slice(n * 2 * HEAD_DIM, (n + 1) * 2 * HEAD_DIM) for n in range(2 * ATTN_KV_HEADS)]
    k_pairs = [block_diag(kv[:, lanes[kvh]]) for kvh in range(ATTN_KV_HEADS)]
    v_pairs = [block_diag(kv[:, lanes[ATTN_KV_HEADS + kvh]]) for kvh in range(ATTN_KV_HEADS)]
    scores = []
    for pair in pairs:
        q2 = q_ref[:, lanes[pair]] * (HEAD_DIM ** -0.5)
        s = jax.lax.dot_general(q2, k_pairs[pair // pairs_per_kv], (((1,), (1,)), ((), ())),
                                preferred_element_type=F32)
        scores.append(s + bias_ref[0, pair])
    probs, scales = [], []
    for pair in pairs:
        halves, inv = [], []
        for half in range(2):
            sh = scores[pair][:, half * 2 * ATTN_BLOCK:(half + 1) * 2 * ATTN_BLOCK]
            sink = sink_ref[2 * pair + half]
            m = jnp.maximum(jnp.max(sh, axis=-1, keepdims=True), sink)
            p = jnp.exp(sh - m)
            inv.append(1.0 / (jnp.sum(p, axis=-1, keepdims=True) + jnp.exp(sink - m)))
            halves.append(p.astype(BF16))
        probs.append(jnp.concatenate(halves, axis=1))
        scales.append(jnp.where(low_out, inv[0], inv[1]))
    for pair in pairs:
        o = jnp.dot(probs[pair], v_pairs[pair // pairs_per_kv], preferred_element_type=F32)
        o_ref[:, lanes[pair]] = (o * scales[pair]).astype(o_ref.dtype)


def _attn_bias():
    qi = jnp.arange(ATTN_BLOCK)[:, None]
    sj = jnp.arange(2 * ATTN_BLOCK)[None, :]
    dist = qi + ATTN_BLOCK - sj
    in_window = (dist >= 0) & (dist < ATTN_BLOCK)
    slopes = 2.0 ** (-8.0 * jnp.arange(1, ATTN_HEADS + 1, dtype=F32) / ATTN_HEADS)
    alibi = -slopes[:, None, None] * dist.astype(F32)
    valid = jnp.stack([in_window & (sj >= ATTN_BLOCK), in_window])[:, None]
    bias = jnp.where(valid, alibi[None], NEG_BIG)
    bias = bias.reshape(2, ATTN_HEADS // 2, 2, ATTN_BLOCK, 2 * ATTN_BLOCK).transpose(0, 1, 3, 2, 4)
    return bias.reshape(2, ATTN_HEADS // 2, ATTN_BLOCK, 4 * ATTN_BLOCK)


def _attn(q, kv, sinks, blocks_per_seq):
    rows = q.shape[0]
    bias = _attn_bias()
    return pl.pallas_call(
        _attn_kernel,
        grid=(rows // ATTN_BLOCK // blocks_per_seq, blocks_per_seq),
        in_specs=[pl.BlockSpec((ATTN_BLOCK, ATTN_DIM), lambda b, i: (b * blocks_per_seq + i, 0)),
                  pl.BlockSpec((ATTN_BLOCK, 4 * KV_DIM), lambda b, i: (b * blocks_per_seq + i, 0)),
                  pl.BlockSpec((ATTN_BLOCK, 4 * KV_DIM),
                               lambda b, i: (b * blocks_per_seq + jnp.maximum(i - 1, 0), 0)),
                  pl.BlockSpec((1,) + bias.shape[1:], lambda b, i: (jnp.minimum(i, 1), 0, 0, 0)),
                  pl.BlockSpec(memory_space=pltpu.SMEM)],
        out_specs=pl.BlockSpec((ATTN_BLOCK, ATTN_DIM), lambda b, i: (b * blocks_per_seq + i, 0)),
        out_shape=jax.ShapeDtypeStruct((rows, ATTN_DIM), BF16),
        compiler_params=_params("arbitrary", "arbitrary"),
        name="swa_attn",
    )(q, kv, kv, bias, sinks)


def _merge_kernel(x_ref, mod_ref, yn_ref, bonus_ref, g_ref, yb_ref, ga_ref, gb_ref, gn_ref, n2_ref,
                  wa_ref, wb_ref, wo_ref, h_ref, u_ref):
    ya = ((yn_ref[...] * gn_ref[0:1, :] + gn_ref[1:2, :]) + bonus_ref[...]) * g_ref[...]
    pa = _bdot(ya, wa_ref[...])
    pb = jnp.dot(yb_ref[...], wb_ref[...], preferred_element_type=F32)
    merged = (jax.nn.sigmoid(ga_ref[...].astype(F32)) * pa
              + jax.nn.sigmoid(gb_ref[...].astype(F32)) * pb)
    h = x_ref[...] + mod_ref[0, 2:3, :] * _bdot(merged, wo_ref[...])
    h_ref[...] = h
    u_ref[...] = _rms_mod(h, n2_ref[...], mod_ref[0, 4:5, :], mod_ref[0, 3:4, :]).astype(BF16)


def _merge(x2, mod3, yn, bonus, g, yb, ga, gb, gn, norm2_g, wa, wb, wo, tiles_per_seq):
    rows = x2.shape[0]

    def tile(width):
        return pl.BlockSpec((ROW_TILE, width), lambda i: (i, 0))

    return pl.pallas_call(
        _merge_kernel,
        grid=(rows // ROW_TILE,),
        in_specs=[tile(D_MODEL),
                  pl.BlockSpec((1, 6, D_MODEL), lambda i: (i // tiles_per_seq, 0, 0)),
                  tile(RWKV_DIM), tile(RWKV_DIM), tile(RWKV_DIM), tile(ATTN_DIM),
                  tile(D_MODEL), tile(D_MODEL),
                  _resident(gn.shape), _resident((1, D_MODEL)),
                  _resident(wa.shape), _resident(wb.shape), _resident(wo.shape)],
        out_specs=[tile(D_MODEL), tile(D_MODEL)],
        out_shape=[jax.ShapeDtypeStruct((rows, D_MODEL), F32),
                   jax.ShapeDtypeStruct((rows, D_MODEL), BF16)],
        compiler_params=_params("arbitrary"),
        name="merge",
    )(x2, mod3, yn, bonus, g, yb, ga, gb, gn, norm2_g.reshape(1, D_MODEL), wa, wb, wo)


def _ffn_kernel(tiles_per_seq, h_ref, u_ref, mod_ref, wg_ref, wv_ref, cw_ref, cb_ref, wd_ref, fg_ref,
                o_ref, carry_ref, act_ref):
    @pl.when(pl.program_id(0) % tiles_per_seq == 0)
    def _():
        carry_ref[...] = jnp.zeros_like(carry_ref)

    u = u_ref[...]
    row = jax.lax.broadcasted_iota(jnp.int32, (ROW_TILE, FFN_CHUNK), 0)
    chunks =[slice(c * FFN_CHUNK, (c + 1) * FFN_CHUNK) for c in range(D_FF // FFN_CHUNK)]

    def up(cols):
        return (jnp.dot(u, wg_ref[:, cols], preferred_element_type=F32),
                jnp.dot(u, wv_ref[:, cols], preferred_element_type=F32))

    ahead = [up(cols) for cols in chunks[:FFN_LOOKAHEAD]]
    for c, cols in enumerate(chunks):
        gate, val = ahead.pop(0)
        if c + FFN_LOOKAHEAD < len(chunks):
            ahead.append(up(chunks[c + FFN_LOOKAHEAD]))
        back1 = carry_ref[SUBLANES - 1:SUBLANES, cols]
        back2 = carry_ref[SUBLANES - 2:SUBLANES - 1, cols]
        carry_ref[:, cols] = gate[ROW_TILE - SUBLANES:, :]
        shift1 = jnp.where(row == 0, back1, pltpu.roll(gate, 1, 0))
        shift2 = jnp.where(row == 0, back2, jnp.where(row == 1, back1, pltpu.roll(gate, 2, 0)))
        conv = (cw_ref[0:1, cols] * shift2 + cw_ref[1:2, cols] * shift1 + cw_ref[2:3, cols] * gate
                + cb_ref[:, cols])
        act_ref[:, cols] = (conv * jax.nn.sigmoid(conv) * val).astype(BF16)
    acc = jnp.dot(act_ref[...], wd_ref[...], preferred_element_type=F32)
    h = h_ref[...] + mod_ref[0, 5:6, :] * acc
    o_ref[...] = h * jax.lax.rsqrt(jnp.mean(h * h, axis=-1, keepdims=True) + RMS_EPS) * fg_ref[...]


def _ffn(h1, u2, mod3, w_gate, w_val, conv_w, conv_b, w_down, final_g, tiles_per_seq):
    rows = h1.shape[0]
    return pl.pallas_call(
        functools.partial(_ffn_kernel, tiles_per_seq),
        grid=(rows // ROW_TILE,),
        in_specs=[pl.BlockSpec((ROW_TILE, D_MODEL), lambda i: (i, 0)),
                  pl.BlockSpec((ROW_TILE, D_MODEL), lambda i: (i, 0)),
                  pl.BlockSpec((1, 6, D_MODEL), lambda i: (i // tiles_per_seq, 0, 0)),
                  _resident(w_gate.shape), _resident(w_val.shape), _resident(conv_w.shape),
                  _resident((1, D_FF)), _resident(w_down.shape), _resident((1, D_MODEL))],
        out_specs=pl.BlockSpec((ROW_TILE, D_MODEL), lambda i: (i, 0)),
        out_shape=jax.ShapeDtypeStruct((rows, D_MODEL), F32),
        scratch_shapes=[pltpu.VMEM((SUBLANES, D_FF), F32), pltpu.VMEM((ROW_TILE, D_FF), BF16)],
        compiler_params=_params("arbitrary"),
        name="ffn",
    )(h1, u2, mod3, w_gate, w_val, conv_w, conv_b.reshape(1, D_FF), w_down, final_g.reshape(1, D_MODEL))


def kernel(x, c, ada_w, ada_b, norm1_g, w_in, rwkv_mu, rwkv_w0, rwkv_w_up, rwkv_a0, rwkv_a_up, rwkv_g_up, rwkv_k_k, rwkv_k_a, rwkv_r_k, rwkv_gn_w, rwkv_gn_b, attn_sinks, w_branch_a, w_branch_b, w_out, norm2_g, ffn_w_up, ffn_conv_w, ffn_conv_b, ffn_w_down, final_g):
    bsz, seq, _ = x.shape
    depth = ada_w.shape[0]
    assert depth == 1, "the ffn kernel fuses the final norm, which follows the last layer only"
    assert bsz * RWKV_HEADS == 128 and seq % ROW_TILE == 0 and seq % SCAN_STEPS == 0
    tiles_per_seq = seq // ROW_TILE
    ones_bd = jnp.kron(jnp.eye(RWKV_HEADS, dtype=F32), jnp.ones((HEAD_DIM, HEAD_DIM), F32)).astype(BF16)
    h = x.reshape(bsz * seq, D_MODEL)
    for l in range(depth):
        mod3 = _mod(c, ada_w[l], ada_b[l]).reshape(bsz, 6, D_MODEL)
        kv0 = RWKV_COLS + ATTN_DIM
        kv_cols = kv0 + (jnp.arange(4 * KV_DIM) // (2 * HEAD_DIM)) * HEAD_DIM + jnp.arange(4 * KV_DIM) % HEAD_DIM
        w_proj = jnp.concatenate([w_in[l][:, :kv0], w_in[l][:, kv_cols], w_in[l][:, kv0 + 2 * KV_DIM:]], axis=1)
        f_rwkv, q, kv, gate_a, gate_b = _in_proj(h, mod3, norm1_g[l], w_proj.astype(BF16), tiles_per_seq)

        lora_w = jnp.zeros((LORA_COLS, 2 * RWKV_DIM), F32)
        lora_w = lora_w.at[:DECAY_LORA, :RWKV_DIM].set(rwkv_w_up[l]).at[DECAY_LORA:, RWKV_DIM:].set(rwkv_a_up[l])
        vecs = jnp.stack([rwkv_w0[l], rwkv_a0[l], rwkv_k_k[l], rwkv_k_a[l], rwkv_r_k[l].reshape(RWKV_DIM)])
        scan_in, g, bonus = _prep(f_rwkv, rwkv_mu[l].reshape(1, RWKV_COLS), lora_w, rwkv_g_up[l],
                                  vecs, ones_bd, tiles_per_seq)
        yn = _scan(scan_in.reshape(SCAN_INPUTS, bsz, seq, RWKV_HEADS, HEAD_DIM)).reshape(bsz * seq, RWKV_DIM)

        yb = _attn(q, kv, attn_sinks[l], seq // ATTN_BLOCK)

        gn = jnp.stack([rwkv_gn_w[l], rwkv_gn_b[l]])
        h, u2 = _merge(h, mod3, yn, bonus, g, yb, gate_a, gate_b, gn, norm2_g[l],
                       w_branch_a[l].astype(BF16), w_branch_b[l].astype(BF16), w_out[l].astype(BF16),
                       tiles_per_seq)
        w_up = ffn_w_up[l].astype(BF16)
        h = _ffn(h, u2, mod3, w_up[:, :D_FF], w_up[:, D_FF:], ffn_conv_w[l], ffn_conv_b[l],
                 ffn_w_down[l].astype(BF16), final_g, tiles_per_seq)
    return h.reshape(bsz, seq, D_MODEL)
```

```python
import functools

import jax
import jax.numpy as jnp
from jax.experimental import pallas as pl
from jax.experimental.pallas import tpu as pltpu

F32 = jnp.float32
BF16 = jnp.bfloat16

D_MODEL = 1024
RWKV_HEADS = 8
HEAD_DIM = 64
RWKV_DIM = RWKV_HEADS * HEAD_DIM
DECAY_LORA = 64
AAA_LORA = 64
GATE_LORA = 128
LORA_COLS = DECAY_LORA + AAA_LORA
RWKV_COLS = 3 * RWKV_DIM + LORA_COLS + GATE_LORA
ATTN_HEADS = 8
ATTN_KV_HEADS = 2
ATTN_GROUP = ATTN_HEADS // ATTN_KV_HEADS
ATTN_DIM = ATTN_HEADS * HEAD_DIM
KV_DIM = ATTN_KV_HEADS * HEAD_DIM
ATTN_BLOCK = 128
D_FF = 2816
GN_EPS = 64e-5
RMS_EPS = 1e-6
DECAY_SCALE = 0.6065306597126334

VMEM_LIMIT_BYTES = 56 * 1024 * 1024
ROW_TILE = 512
FFN_CHUNK = 256
FFN_LOOKAHEAD = 2
SCAN_STEPS = 32
SCAN_INPUTS = 6
UPDATE_GROUPS = 2
SUBLANES = 8
NEG_BIG = -1e30


def _params(*semantics):
    return pltpu.CompilerParams(dimension_semantics=semantics, vmem_limit_bytes=VMEM_LIMIT_BYTES)


def _resident(shape):
    zeros = (0,) * len(shape)
    return pl.BlockSpec(shape, lambda *_: zeros, pipeline_mode=pl.Buffered(1))


def _bdot(a, b):
    return jnp.dot(a.astype(BF16), b.astype(BF16), preferred_element_type=F32)


def _split(x):
    hi = x.astype(BF16)
    lo = (x - hi.astype(F32)).astype(BF16)
    return hi, lo


def _dot3(a, b):
    a_hi, a_lo = _split(a)
    b_hi, b_lo = _split(b)
    dot = functools.partial(jnp.dot, preferred_element_type=F32)
    return dot(a_hi, b_hi) + dot(a_lo, b_hi) + dot(a_hi, b_lo)


def _head_sum(x, ones_blockdiag):
    hi, lo = _split(x)
    dot = functools.partial(jnp.dot, preferred_element_type=F32)
    return dot(hi, ones_blockdiag) + dot(lo, ones_blockdiag)


def _rms_mod(h, gain, scale, shift):
    y = h * jax.lax.rsqrt(jnp.mean(h * h, axis=-1, keepdims=True) + RMS_EPS)
    return y * gain * (1.0 + scale) + shift


def _mod_kernel(c_ref, w_ref, b_ref, o_ref):
    c = c_ref[...]
    o_ref[...] = _bdot(c * jax.nn.sigmoid(c), w_ref[...]) + b_ref[...]


def _mod(c, ada_w, ada_b):
    bsz = c.shape[0]
    cols = ada_w.shape[1]
    tile = cols // 4
    return pl.pallas_call(
        _mod_kernel,
        grid=(cols // tile,),
        in_specs=[pl.BlockSpec((bsz, D_MODEL), lambda j: (0, 0)),
                  pl.BlockSpec((D_MODEL, tile), lambda j: (0, j)),
                  pl.BlockSpec((1, tile), lambda j: (0, j))],
        out_specs=pl.BlockSpec((bsz, tile), lambda j: (0, j)),
        out_shape=jax.ShapeDtypeStruct((bsz, cols), F32),
        compiler_params=_params("arbitrary"),
        name="mod",
    )(c, ada_w, ada_b.reshape(1, cols))


def _in_proj_kernel(x_ref, mod_ref, g_ref, w_ref, f_ref, q_ref, kv_ref, ga_ref, gb_ref):
    u = _rms_mod(x_ref[...], g_ref[...], mod_ref[0, 1:2, :], mod_ref[0, 0:1, :]).astype(BF16)
    col = 0
    for out in (f_ref, q_ref, kv_ref, ga_ref, gb_ref):
        width = out.shape[1]
        out[...] = jnp.dot(u, w_ref[:, col:col + width], preferred_element_type=F32).astype(out.dtype)
        col += width


def _in_proj(x2, mod3, norm_g, w_in, tiles_per_seq):
    rows = x2.shape[0]
    widths = (RWKV_COLS, ATTN_DIM, 4 * KV_DIM, D_MODEL, D_MODEL)
    dtypes = (F32, BF16, BF16, BF16, BF16)
    return pl.pallas_call(
        _in_proj_kernel,
        grid=(rows // ROW_TILE,),
        in_specs=[pl.BlockSpec((ROW_TILE, D_MODEL), lambda i: (i, 0)),
                  pl.BlockSpec((1, 6, D_MODEL), lambda i: (i // tiles_per_seq, 0, 0)),
                  _resident((1, D_MODEL)),
                  _resident(w_in.shape)],
        out_specs=[pl.BlockSpec((ROW_TILE, w), lambda i: (i, 0)) for w in widths],
        out_shape=[jax.ShapeDtypeStruct((rows, w), d) for w, d in zip(widths, dtypes)],
        compiler_params=_params("arbitrary"),
        name="in_proj",
    )(x2, mod3, norm_g.reshape(1, D_MODEL), w_in)


def _prep_kernel(tiles_per_seq, f_ref, prev_ref, mu_ref, lora_ref, gup_ref, vec_ref, ones_ref,
                 x_ref, g_ref, bonus_ref):
    feat = f_ref[...]
    seq_start = pl.program_id(0) % tiles_per_seq == 0
    prev_row = jnp.where(seq_start, 0.0, prev_ref[SUBLANES - 1:SUBLANES, :])
    row = jax.lax.broadcasted_iota(jnp.int32, feat.shape, 0)
    shifted = jnp.where(row == 0, prev_row, pltpu.roll(feat, 1, 0))
    f = feat + (shifted - feat) * mu_ref[...]

    r = f[:, 0:RWKV_DIM]
    k = f[:, RWKV_DIM:2 * RWKV_DIM]
    v = f[:, 2 * RWKV_DIM:3 * RWKV_DIM]
    lo = f[:, 3 * RWKV_DIM:3 * RWKV_DIM + LORA_COLS]
    g_lo = f[:, 3 * RWKV_DIM + LORA_COLS:]
    w0, a0, k_k, k_a, r_k = (vec_ref[i:i + 1, :] for i in range(5))

    lane = jax.lax.broadcasted_iota(jnp.int32, lo.shape, 1)
    lora = _dot3(jnp.where(lane < DECAY_LORA, jnp.tanh(lo), lo), lora_ref[...])
    decay = jnp.exp(-DECAY_SCALE * jax.nn.sigmoid(w0 + lora[:, :RWKV_DIM]))
    a = jax.nn.sigmoid(a0 + lora[:, RWKV_DIM:])
    g = _bdot(jax.nn.sigmoid(g_lo), gup_ref[...])

    ones = ones_ref[...]
    kk = k * k_k
    kk = kk / jnp.maximum(jnp.sqrt(_head_sum(kk * kk, ones)), 1e-12)
    k = k * (1.0 + (a - 1.0) * k_a)

    for i, t in enumerate((r, decay, k, v, kk, kk * a)):
        x_ref[i] = t
    g_ref[...] = g
    bonus_ref[...] = _head_sum(r * k * r_k, ones) * v


def _prep(f_rwkv, mu, lora_w, g_up, vecs, ones_bd, tiles_per_seq):
    rows = f_rwkv.shape[0]
    prev_blocks = ROW_TILE // SUBLANES
    out = pl.BlockSpec((ROW_TILE, RWKV_DIM), lambda i: (i, 0))
    return pl.pallas_call(
        functools.partial(_prep_kernel, tiles_per_seq),
        grid=(rows // ROW_TILE,),
        in_specs=[pl.BlockSpec((ROW_TILE, RWKV_COLS), lambda i: (i, 0)),
                  pl.BlockSpec((SUBLANES, RWKV_COLS), lambda i: (jnp.maximum(i * prev_blocks - 1, 0), 0)),
                  _resident(mu.shape), _resident(lora_w.shape), _resident(g_up.shape),
                  _resident(vecs.shape), _resident(ones_bd.shape)],
        out_specs=[pl.BlockSpec((SCAN_INPUTS, ROW_TILE, RWKV_DIM), lambda i: (0, i, 0)), out, out],
        out_shape=[jax.ShapeDtypeStruct((SCAN_INPUTS, rows, RWKV_DIM), F32)]
        + [jax.ShapeDtypeStruct((rows, RWKV_DIM), F32)] * 2,
        compiler_params=_params("arbitrary"),
        name="rwkv_prep",
    )(f_rwkv, f_rwkv, mu, lora_w, g_up, vecs, ones_bd)


def _scan_kernel(x_ref, perm_ref, y_ref, state_ref, rows_even_ref, rows_odd_ref, yraw_ref, xtime_ref,
                 ystage_ref):
    pairs = state_ref.shape[-1]
    bsz = x_ref.shape[1]
    groups = x_ref.shape[2]
    two_heads = 2 * HEAD_DIM
    R, W, K, V, KK, B = range(SCAN_INPUTS)
    per_group = HEAD_DIM // UPDATE_GROUPS

    def permute_rows(p, x):
        hi = x.astype(BF16)
        rest = x - hi.astype(F32)
        mid = rest.astype(BF16)
        lo = (rest - mid.astype(F32)).astype(BF16)
        dot = functools.partial(jnp.dot, preferred_element_type=F32)
        return dot(p, hi) + dot(p, mid) + dot(p, lo)

    def stage_in(i, group, parity):
        x = x_ref[i, :, group].reshape(bsz * SUBLANES, RWKV_DIM)
        xtime_ref[parity, i] = permute_rows(perm_ref[0], x)

    def transpose_in(i, parity, slot, rows_ref):
        slab = jnp.concatenate(
            [xtime_ref[parity, i, slot * bsz:(slot + 1) * bsz, hp * two_heads:(hp + 1) * two_heads]
             for hp in range(RWKV_HEADS // 2)], axis=0)
        xt = slab.T
        xt = jnp.concatenate([xt[:HEAD_DIM], xt[HEAD_DIM:]], axis=1)
        rows_ref[i] = xt.reshape(SUBLANES, SUBLANES, pairs)

    def write_out(slot):
        y = yraw_ref[...]
        cen = y - jnp.mean(y, axis=0, keepdims=True)
        var = jnp.mean(cen * cen, axis=0, keepdims=True)
        yn = cen * jax.lax.rsqrt(var + GN_EPS)
        slab = jnp.concatenate([yn[:, :pairs // 2], yn[:, pairs // 2:]], axis=0).T
        for hp in range(RWKV_HEADS // 2):
            ystage_ref[slot * bsz:(slot + 1) * bsz, hp * two_heads:(hp + 1) * two_heads] = (
                slab[hp * bsz:(hp + 1) * bsz])

    def flush(group):
        y = permute_rows(perm_ref[1], ystage_ref[...])
        y_ref[:, group] = y.reshape(bsz, SUBLANES, RWKV_DIM)

    @pl.when(pl.program_id(0) == 0)
    def _():
        state_ref[...] = jnp.zeros_like(state_ref)
        yraw_ref[...] = jnp.zeros_like(yraw_ref)
        ystage_ref[...] = jnp.zeros_like(ystage_ref)

    for i in range(SCAN_INPUTS):
        stage_in(i, 0, 0)
        transpose_in(i, 0, 0, rows_even_ref)

    def step(group, slot, rows_ref, next_rows_ref):
        parity = group % 2
        write_out((slot - 1) % SUBLANES)
        if slot == 0:
            flush(jnp.maximum(group - 1, 0))
        if slot < SCAN_INPUTS:
            stage_in(slot, jnp.minimum(group + 1, groups - 1), 1 - parity)
        next_step = (parity, slot + 1) if slot + 1 < SUBLANES else (1 - parity, 0)

        acc = jnp.zeros((HEAD_DIM, pairs), F32)
        for j in range(HEAD_DIM):
            acc = acc + state_ref[j] * rows_ref[KK, j // SUBLANES, j % SUBLANES:j % SUBLANES + 1, :]
        sa = -acc
        v = rows_ref[V].reshape(HEAD_DIM, pairs)

        def update(grp, y):
            for u in range(SCAN_INPUTS // UPDATE_GROUPS):
                transpose_in(grp * (SCAN_INPUTS // UPDATE_GROUPS) + u, *next_step, next_rows_ref)
            for jj in range(per_group):
                j = grp * per_group + jj
                jh = grp * (per_group // SUBLANES) + jj // SUBLANES
                kl = slice(jj % SUBLANES, jj % SUBLANES + 1)
                s_new = (state_ref[j] * rows_ref[W, jh, kl, :] + sa * rows_ref[B, jh, kl, :]
                         + v * rows_ref[K, jh, kl, :])
                state_ref[j] = s_new
                y = y + s_new * rows_ref[R, jh, kl, :]
            return y

        yraw_ref[...] = jax.lax.fori_loop(0, UPDATE_GROUPS, update, jnp.zeros((HEAD_DIM, pairs), F32))

    def eight_steps(group, carry):
        for slot in range(0, SUBLANES, 2):
            step(group, slot, rows_even_ref, rows_odd_ref)
            step(group, slot + 1, rows_odd_ref, rows_even_ref)
        return carry

    jax.lax.fori_loop(0, groups, eight_steps, 0)
    write_out(SUBLANES - 1)
    flush(groups - 1)


def _scan(x):
    bsz, all_groups = x.shape[1:3]
    pairs = bsz * RWKV_HEADS
    groups = SCAN_STEPS // SUBLANES
    rows = bsz * SUBLANES
    to_time = (jnp.arange(rows)[:, None] ==
               (jnp.arange(rows)[None, :] % SUBLANES) * bsz + jnp.arange(rows)[None, :] // SUBLANES)
    perm = jnp.stack([to_time, to_time.T]).astype(BF16)
    return pl.pallas_call(
        _scan_kernel,
        grid=(all_groups // groups,),
        in_specs=[pl.BlockSpec((SCAN_INPUTS, bsz, groups, SUBLANES, RWKV_DIM), lambda i: (0, 0, i, 0, 0)),
                  _resident(perm.shape)],
        out_specs=pl.BlockSpec((bsz, groups, SUBLANES, RWKV_DIM), lambda i: (0, i, 0, 0)),
        out_shape=jax.ShapeDtypeStruct((bsz, all_groups, SUBLANES, RWKV_DIM), F32),
        scratch_shapes=[pltpu.VMEM((HEAD_DIM, HEAD_DIM, pairs), F32),
                        pltpu.VMEM((SCAN_INPUTS, SUBLANES, SUBLANES, pairs), F32),
                        pltpu.VMEM((SCAN_INPUTS, SUBLANES, SUBLANES, pairs), F32),
                        pltpu.VMEM((HEAD_DIM, pairs), F32),
                        pltpu.VMEM((2, SCAN_INPUTS, rows, RWKV_DIM), F32),
                        pltpu.VMEM((rows, RWKV_DIM), F32)],
        compiler_params=_params("arbitrary"),
        name="rwkv_scan",
    )(x, perm)


def _attn_kernel(q_ref, kv_ref, kvp_ref, bias_ref, sink_ref, o_ref):
    kv = jnp.concatenate([kvp_ref[...], kv_ref[...]], axis=0)
    low = jax.lax.broadcasted_iota(jnp.int32, (2 * ATTN_BLOCK, 2 * HEAD_DIM), 1) < HEAD_DIM
    low_out = jax.lax.broadcasted_iota(jnp.int32, (ATTN_BLOCK, 2 * HEAD_DIM), 1) < HEAD_DIM
    nothing = jnp.zeros((2 * ATTN_BLOCK, 2 * HEAD_DIM), BF16)

    def block_diag(t):
        return jnp.concatenate([jnp.where(low, t, nothing), jnp.where(low, nothing, t)], axis=0)

    pairs = range(ATTN_HEADS // 2)
    pairs_per_kv = ATTN_GROUP // 2
    lanes = [slice(n * 2 * HEAD_DIM, (n + 1) * 2 * HEAD_DIM) for n in range(2 * ATTN_KV_HEADS)]
    k_pairs = [block_diag(kv[:, lanes[kvh]]) for kvh in range(ATTN_KV_HEADS)]
    v_pairs = [block_diag(kv[:, lanes[ATTN_KV_HEADS + kvh]]) for kvh in range(ATTN_KV_HEADS)]
    scores = []
    for pair in pairs:
        q2 = q_ref[:, lanes[pair]] * (HEAD_DIM ** -0.5)
        s = jax.lax.dot_general(q2, k_pairs[pair // pairs_per_kv], (((1,), (1,)), ((), ())),
                                preferred_element_type=F32)
        scores.append(s + bias_ref[0, pair])
    probs, scales = [], []
    for pair in pairs:
        halves, inv = [], []
        for half in range(2):
            sh = scores[pair][:, half * 2 * ATTN_BLOCK:(half + 1) * 2 * ATTN_BLOCK]
            sink = sink_ref[2 * pair + half]
            m = jnp.maximum(jnp.max(sh, axis=-1, keepdims=True), sink)
            p = jnp.exp(sh - m)
            inv.append(1.0 / (jnp.sum(p, axis=-1, keepdims=True) + jnp.exp(sink - m)))
            halves.append(p.astype(BF16))
        probs.append(jnp.concatenate(halves, axis=1))
        scales.append(jnp.where(low_out, inv[0], inv[1]))
    for pair in pairs:
        o = jnp.dot(probs[pair], v_pairs[pair // pairs_per_kv], preferred_element_type=F32)
        o_ref[:, lanes[pair]] = (o * scales[pair]).astype(o_ref.dtype)


def _attn_bias():
    qi = jnp.arange(ATTN_BLOCK)[:, None]
    sj = jnp.arange(2 * ATTN_BLOCK)[None, :]
    dist = qi + ATTN_BLOCK - sj
    in_window = (dist >= 0) & (dist < ATTN_BLOCK)
    slopes = 2.0 ** (-8.0 * jnp.arange(1, ATTN_HEADS + 1, dtype=F32) / ATTN_HEADS)
    alibi = -slopes[:, None, None] * dist.astype(F32)
    valid = jnp.stack([in_window & (sj >= ATTN_BLOCK), in_window])[:, None]
    bias = jnp.where(valid, alibi[None], NEG_BIG)
    bias = bias.reshape(2, ATTN_HEADS // 2, 2, ATTN_BLOCK, 2 * ATTN_BLOCK).transpose(0, 1, 3, 2, 4)
    return bias.reshape(2, ATTN_HEADS // 2, ATTN_BLOCK, 4 * ATTN_BLOCK)


def _attn(q, kv, sinks, blocks_per_seq):
    rows = q.shape[0]
    bias = _attn_bias()
    return pl.pallas_call(
        _attn_kernel,
        grid=(rows // ATTN_BLOCK // blocks_per_seq, blocks_per_seq),
        in_specs=[pl.BlockSpec((ATTN_BLOCK, ATTN_DIM), lambda b, i: (b * blocks_per_seq + i, 0)),
                  pl.BlockSpec((ATTN_BLOCK, 4 * KV_DIM), lambda b, i: (b * blocks_per_seq + i, 0)),
                  pl.BlockSpec((ATTN_BLOCK, 4 * KV_DIM),
                               lambda b, i: (b * blocks_per_seq + jnp.maximum(i - 1, 0), 0)),
                  pl.BlockSpec((1,) + bias.shape[1:], lambda b, i: (jnp.minimum(i, 1), 0, 0, 0)),
                  pl.BlockSpec(memory_space=pltpu.SMEM)],
        out_specs=pl.BlockSpec((ATTN_BLOCK, ATTN_DIM), lambda b, i: (b * blocks_per_seq + i, 0)),
        out_shape=jax.ShapeDtypeStruct((rows, ATTN_DIM), BF16),
        compiler_params=_params("arbitrary", "arbitrary"),
        name="swa_attn",
    )(q, kv, kv, bias, sinks)


def _merge_kernel(x_ref, mod_ref, yn_ref, bonus_ref, g_ref, yb_ref, ga_ref, gb_ref, gn_ref, n2_ref,
                  wa_ref, wb_ref, wo_ref, h_ref, u_ref):
    ya = ((yn_ref[...] * gn_ref[0:1, :] + gn_ref[1:2, :]) + bonus_ref[...]) * g_ref[...]
    pa = _bdot(ya, wa_ref[...])
    pb = jnp.dot(yb_ref[...], wb_ref[...], preferred_element_type=F32)
    merged = (jax.nn.sigmoid(ga_ref[...].astype(F32)) * pa
              + jax.nn.sigmoid(gb_ref[...].astype(F32)) * pb)
    h = x_ref[...] + mod_ref[0, 2:3, :] * _bdot(merged, wo_ref[...])
    h_ref[...] = h
    u_ref[...] = _rms_mod(h, n2_ref[...], mod_ref[0, 4:5, :], mod_ref[0, 3:4, :]).astype(BF16)


def _merge(x2, mod3, yn, bonus, g, yb, ga, gb, gn, norm2_g, wa, wb, wo, tiles_per_seq):
    rows = x2.shape[0]

    def tile(width):
        return pl.BlockSpec((ROW_TILE, width), lambda i: (i, 0))

    return pl.pallas_call(
        _merge_kernel,
        grid=(rows // ROW_TILE,),
        in_specs=[tile(D_MODEL),
                  pl.BlockSpec((1, 6, D_MODEL), lambda i: (i // tiles_per_seq, 0, 0)),
                  tile(RWKV_DIM), tile(RWKV_DIM), tile(RWKV_DIM), tile(ATTN_DIM),
                  tile(D_MODEL), tile(D_MODEL),
                  _resident(gn.shape), _resident((1, D_MODEL)),
                  _resident(wa.shape), _resident(wb.shape), _resident(wo.shape)],
        out_specs=[tile(D_MODEL), tile(D_MODEL)],
        out_shape=[jax.ShapeDtypeStruct((rows, D_MODEL), F32),
                   jax.ShapeDtypeStruct((rows, D_MODEL), BF16)],
        compiler_params=_params("arbitrary"),
        name="merge",
    )(x2, mod3, yn, bonus, g, yb, ga, gb, gn, norm2_g.reshape(1, D_MODEL), wa, wb, wo)


def _ffn_kernel(tiles_per_seq, h_ref, u_ref, mod_ref, wg_ref, wv_ref, cw_ref, cb_ref, wd_ref, fg_ref,
                o_ref, carry_ref, act_ref):
    @pl.when(pl.program_id(0) % tiles_per_seq == 0)
    def _():
        carry_ref[...] = jnp.zeros_like(carry_ref)

    u = u_ref[...]
    row = jax.lax.broadcasted_iota(jnp.int32, (ROW_TILE, FFN_CHUNK), 0)
    chunks =[slice(c * FFN_CHUNK, (c + 1) * FFN_CHUNK) for c in range(D_FF // FFN_CHUNK)]

    def up(cols):
        return (jnp.dot(u, wg_ref[:, cols], preferred_element_type=F32),
                jnp.dot(u, wv_ref[:, cols], preferred_element_type=F32))

    ahead = [up(cols) for cols in chunks[:FFN_LOOKAHEAD]]
    for c, cols in enumerate(chunks):
        gate, val = ahead.pop(0)
        if c + FFN_LOOKAHEAD < len(chunks):
            ahead.append(up(chunks[c + FFN_LOOKAHEAD]))
        back1 = carry_ref[SUBLANES - 1:SUBLANES, cols]
        back2 = carry_ref[SUBLANES - 2:SUBLANES - 1, cols]
        carry_ref[:, cols] = gate[ROW_TILE - SUBLANES:, :]
        shift1 = jnp.where(row == 0, back1, pltpu.roll(gate, 1, 0))
        shift2 = jnp.where(row == 0, back2, jnp.where(row == 1, back1, pltpu.roll(gate, 2, 0)))
        conv = (cw_ref[0:1, cols] * shift2 + cw_ref[1:2, cols] * shift1 + cw_ref[2:3, cols] * gate
                + cb_ref[:, cols])
        act_ref[:, cols] = (conv * jax.nn.sigmoid(conv) * val).astype(BF16)
    acc = jnp.dot(act_ref[...], wd_ref[...], preferred_element_type=F32)
    h = h_ref[...] + mod_ref[0, 5:6, :] * acc
    o_ref[...] = h * jax.lax.rsqrt(jnp.mean(h * h, axis=-1, keepdims=True) + RMS_EPS) * fg_ref[...]


def _ffn(h1, u2, mod3, w_gate, w_val, conv_w, conv_b, w_down, final_g, tiles_per_seq):
    rows = h1.shape[0]
    return pl.pallas_call(
        functools.partial(_ffn_kernel, tiles_per_seq),
        grid=(rows // ROW_TILE,),
        in_specs=[pl.BlockSpec((ROW_TILE, D_MODEL), lambda i: (i, 0)),
                  pl.BlockSpec((ROW_TILE, D_MODEL), lambda i: (i, 0)),
                  pl.BlockSpec((1, 6, D_MODEL), lambda i: (i // tiles_per_seq, 0, 0)),
                  _resident(w_gate.shape), _resident(w_val.shape), _resident(conv_w.shape),
                  _resident((1, D_FF)), _resident(w_down.shape), _resident((1, D_MODEL))],
        out_specs=pl.BlockSpec((ROW_TILE, D_MODEL), lambda i: (i, 0)),
        out_shape=jax.ShapeDtypeStruct((rows, D_MODEL), F32),
        scratch_shapes=[pltpu.VMEM((SUBLANES, D_FF), F32), pltpu.VMEM((ROW_TILE, D_FF), BF16)],
        compiler_params=_params("arbitrary"),
        name="ffn",
    )(h1, u2, mod3, w_gate, w_val, conv_w, conv_b.reshape(1, D_FF), w_down, final_g.reshape(1, D_MODEL))


def kernel(x, c, ada_w, ada_b, norm1_g, w_in, rwkv_mu, rwkv_w0, rwkv_w_up, rwkv_a0, rwkv_a_up, rwkv_g_up, rwkv_k_k, rwkv_k_a, rwkv_r_k, rwkv_gn_w, rwkv_gn_b, attn_sinks, w_branch_a, w_branch_b, w_out, norm2_g, ffn_w_up, ffn_conv_w, ffn_conv_b, ffn_w_down, final_g):
    bsz, seq, _ = x.shape
    depth = ada_w.shape[0]
    assert depth == 1, "the ffn kernel fuses the final norm, which follows the last layer only"
    assert bsz * RWKV_HEADS == 128 and seq % ROW_TILE == 0 and seq % SCAN_STEPS == 0
    tiles_per_seq = seq // ROW_TILE
    ones_bd = jnp.kron(jnp.eye(RWKV_HEADS, dtype=F32), jnp.ones((HEAD_DIM, HEAD_DIM), F32)).astype(BF16)
    h = x.reshape(bsz * seq, D_MODEL)
    for l in range(depth):
        mod3 = _mod(c, ada_w[l], ada_b[l]).reshape(bsz, 6, D_MODEL)
        kv0 = RWKV_COLS + ATTN_DIM
        kv_cols = kv0 + (jnp.arange(4 * KV_DIM) // (2 * HEAD_DIM)) * HEAD_DIM + jnp.arange(4 * KV_DIM) % HEAD_DIM
        w_proj = jnp.concatenate([w_in[l][:, :kv0], w_in[l][:, kv_cols], w_in[l][:, kv0 + 2 * KV_DIM:]], axis=1)
        f_rwkv, q, kv, gate_a, gate_b = _in_proj(h, mod3, norm1_g[l], w_proj.astype(BF16), tiles_per_seq)

        lora_w = jnp.zeros((LORA_COLS, 2 * RWKV_DIM), F32)
        lora_w = lora_w.at[:DECAY_LORA, :RWKV_DIM].set(rwkv_w_up[l]).at[DECAY_LORA:, RWKV_DIM:].set(rwkv_a_up[l])
        vecs = jnp.stack([rwkv_w0[l], rwkv_a0[l], rwkv_k_k[l], rwkv_k_a[l], rwkv_r_k[l].reshape(RWKV_DIM)])
        scan_in, g, bonus = _prep(f_rwkv, rwkv_mu[l].reshape(1, RWKV_COLS), lora_w, rwkv_g_up[l],
                                  vecs, ones_bd, tiles_per_seq)
        yn = _scan(scan_in.reshape(SCAN_INPUTS, bsz, seq // SUBLANES, SUBLANES, RWKV_DIM))
        yn = yn.reshape(bsz * seq, RWKV_DIM)

        yb = _attn(q, kv, attn_sinks[l], seq // ATTN_BLOCK)

        gn = jnp.stack([rwkv_gn_w[l], rwkv_gn_b[l]])
        h, u2 = _merge(h, mod3, yn, bonus, g, yb, gate_a, gate_b, gn, norm2_g[l],
                       w_branch_a[l].astype(BF16), w_branch_b[l].astype(BF16), w_out[l].astype(BF16),
                       tiles_per_seq)
        w_up = ffn_w_up[l].astype(BF16)
        h = _ffn(h, u2, mod3, w_up[:, :D_FF], w_up[:, D_FF:], ffn_conv_w[l], ffn_conv_b[l],
                 ffn_w_down[l].astype(BF16), final_g, tiles_per_seq)
    return h.reshape(bsz, seq, D_MODEL)
```

```python
import functools

import jax
import jax.numpy as jnp
from jax.experimental import pallas as pl
from jax.experimental.pallas import tpu as pltpu

F32 = jnp.float32
BF16 = jnp.bfloat16

D_MODEL = 1024
RWKV_HEADS = 8
HEAD_DIM = 64
RWKV_DIM = RWKV_HEADS * HEAD_DIM
DECAY_LORA = 64
AAA_LORA = 64
GATE_LORA = 128
LORA_COLS = DECAY_LORA + AAA_LORA
RWKV_COLS = 3 * RWKV_DIM + LORA_COLS + GATE_LORA
ATTN_HEADS = 8
ATTN_KV_HEADS = 2
ATTN_GROUP = ATTN_HEADS // ATTN_KV_HEADS
ATTN_DIM = ATTN_HEADS * HEAD_DIM
KV_DIM = ATTN_KV_HEADS * HEAD_DIM
ATTN_BLOCK = 128
D_FF = 2816
GN_EPS = 64e-5
RMS_EPS = 1e-6
DECAY_SCALE = 0.6065306597126334

VMEM_LIMIT_BYTES = 56 * 1024 * 1024
ROW_TILE = 512
FFN_CHUNK = 256
FFN_LOOKAHEAD = 2
SCAN_STEPS = 32
SCAN_INPUTS = 6
UPDATE_GROUPS = 2
SUBLANES = 8
NEG_BIG = -1e30


def _params(*semantics):
    return pltpu.CompilerParams(dimension_semantics=semantics, vmem_limit_bytes=VMEM_LIMIT_BYTES)


def _resident(shape):
    zeros = (0,) * len(shape)
    return pl.BlockSpec(shape, lambda *_: zeros, pipeline_mode=pl.Buffered(1))


def _bdot(a, b):
    return jnp.dot(a.astype(BF16), b.astype(BF16), preferred_element_type=F32)


def _split(x):
    hi = x.astype(BF16)
    lo = (x - hi.astype(F32)).astype(BF16)
    return hi, lo


def _dot3(a, b):
    a_hi, a_lo = _split(a)
    b_hi, b_lo = _split(b)
    dot = functools.partial(jnp.dot, preferred_element_type=F32)
    return dot(a_hi, b_hi) + dot(a_lo, b_hi) + dot(a_hi, b_lo)


def _head_sum(x, ones_blockdiag):
    hi, lo = _split(x)
    dot = functools.partial(jnp.dot, preferred_element_type=F32)
    return dot(hi, ones_blockdiag) + dot(lo, ones_blockdiag)


def _rms_mod(h, gain, scale, shift):
    y = h * jax.lax.rsqrt(jnp.mean(h * h, axis=-1, keepdims=True) + RMS_EPS)
    return y * gain * (1.0 + scale) + shift


def _permute_rows(p, x, pieces=3):
    dot = functools.partial(jnp.dot, preferred_element_type=F32)
    out = None
    for _ in range(pieces):
        piece = x.astype(BF16)
        x = x - piece.astype(F32)
        out = dot(p, piece) if out is None else out + dot(p, piece)
    return out


def _row_order_perms(bsz):
    rows = bsz * SUBLANES
    src = jnp.arange(rows)[None, :]
    to_time = jnp.arange(rows)[:, None] == (src % SUBLANES) * bsz + src // SUBLANES
    return jnp.stack([to_time, to_time.T]).astype(BF16)


def _per_batch_rows(t, bsz):
    groups = ROW_TILE // bsz // SUBLANES
    return jnp.broadcast_to(t[:, None], (bsz, groups) + t.shape[1:]).reshape(ROW_TILE, t.shape[-1])


def _mod_kernel(c_ref, w_ref, b_ref, o_ref):
    c = c_ref[...]
    o_ref[...] = _bdot(c * jax.nn.sigmoid(c), w_ref[...]) + b_ref[...]


def _mod(c, ada_w, ada_b):
    bsz = c.shape[0]
    cols = ada_w.shape[1]
    tile = cols // 4
    return pl.pallas_call(
        _mod_kernel,
        grid=(cols // tile,),
        in_specs=[pl.BlockSpec((bsz, D_MODEL), lambda j: (0, 0)),
                  pl.BlockSpec((D_MODEL, tile), lambda j: (0, j)),
                  pl.BlockSpec((1, tile), lambda j: (0, j))],
        out_specs=pl.BlockSpec((bsz, tile), lambda j: (0, j)),
        out_shape=jax.ShapeDtypeStruct((bsz, cols), F32),
        compiler_params=_params("arbitrary"),
        name="mod",
    )(c, ada_w, ada_b.reshape(1, cols))


def _in_proj_kernel(x_ref, mod_ref, g_ref, w_ref, f_ref, q_ref, kv_ref, ga_ref, gb_ref):
    u = _rms_mod(x_ref[...], g_ref[...], mod_ref[0, 1:2, :], mod_ref[0, 0:1, :]).astype(BF16)
    col = 0
    for out in (f_ref, q_ref, kv_ref, ga_ref, gb_ref):
        width = out.shape[1]
        out[...] = jnp.dot(u, w_ref[:, col:col + width], preferred_element_type=F32).astype(out.dtype)
        col += width


def _in_proj(x2, mod3, norm_g, w_in, tiles_per_seq):
    rows = x2.shape[0]
    widths = (RWKV_COLS, ATTN_DIM, 4 * KV_DIM, D_MODEL, D_MODEL)
    dtypes = (F32, BF16, BF16, BF16, BF16)
    return pl.pallas_call(
        _in_proj_kernel,
        grid=(rows // ROW_TILE,),
        in_specs=[pl.BlockSpec((ROW_TILE, D_MODEL), lambda i: (i, 0)),
                  pl.BlockSpec((1, 6, D_MODEL), lambda i: (i // tiles_per_seq, 0, 0)),
                  _resident((1, D_MODEL)),
                  _resident(w_in.shape)],
        out_specs=[pl.BlockSpec((ROW_TILE, w), lambda i: (i, 0)) for w in widths],
        out_shape=[jax.ShapeDtypeStruct((rows, w), d) for w, d in zip(widths, dtypes)],
        compiler_params=_params("arbitrary"),
        name="in_proj",
    )(x2, mod3, norm_g.reshape(1, D_MODEL), w_in)


def _prep_kernel(f_ref, prev_ref, mu_ref, lora_ref, gup_ref, vec_ref, ones_ref, perm_ref,
                 x_ref, g_ref, bonus_ref, group_ref):
    bsz, steps, cols = f_ref.shape
    feat = f_ref[...].reshape(ROW_TILE, cols)
    before = pltpu.roll(prev_ref[...].reshape(bsz * SUBLANES, cols), bsz * SUBLANES - (SUBLANES - 1), 0)
    before = jnp.where(pl.program_id(0) == 0, 0.0, before)
    before = _per_batch_rows(before.reshape(bsz, SUBLANES, cols), bsz)
    row = jax.lax.broadcasted_iota(jnp.int32, feat.shape, 0)
    shifted = jnp.where(row % steps == 0, before, pltpu.roll(feat, 1, 0))
    f = feat + (shifted - feat) * mu_ref[...]

    r = f[:, 0:RWKV_DIM]
    k = f[:, RWKV_DIM:2 * RWKV_DIM]
    v = f[:, 2 * RWKV_DIM:3 * RWKV_DIM]
    lo = f[:, 3 * RWKV_DIM:3 * RWKV_DIM + LORA_COLS]
    g_lo = f[:, 3 * RWKV_DIM + LORA_COLS:]
    w0, a0, k_k, k_a, r_k = (vec_ref[i:i + 1, :] for i in range(5))

    lane = jax.lax.broadcasted_iota(jnp.int32, lo.shape, 1)
    lora = _dot3(jnp.where(lane < DECAY_LORA, jnp.tanh(lo), lo), lora_ref[...])
    decay = jnp.exp(-DECAY_SCALE * jax.nn.sigmoid(w0 + lora[:, :RWKV_DIM]))
    a = jax.nn.sigmoid(a0 + lora[:, RWKV_DIM:])
    g = _bdot(jax.nn.sigmoid(g_lo), gup_ref[...])

    ones = ones_ref[...]
    kk = k * k_k
    kk = kk / jnp.maximum(jnp.sqrt(_head_sum(kk * kk, ones)), 1e-12)
    k = k * (1.0 + (a - 1.0) * k_a)

    to_time = perm_ref[0]
    group_rows = bsz * SUBLANES
    for i, t in enumerate((r, decay, k, v, kk, kk * a)):
        pieces = 3 if i == 1 else 2
        group_ref[...] = t.reshape(bsz, steps // SUBLANES, SUBLANES, RWKV_DIM)
        for grp in range(steps // SUBLANES):
            x_ref[i, grp * group_rows:(grp + 1) * group_rows, :] = _permute_rows(
                to_time, group_ref[:, grp].reshape(group_rows, RWKV_DIM), pieces)
    g_ref[...] = g.reshape(bsz, steps, RWKV_DIM)
    bonus_ref[...] = (_head_sum(r * k * r_k, ones) * v).reshape(bsz, steps, RWKV_DIM)


def _prep(f_rwkv, mu, lora_w, g_up, vecs, ones_bd, perms):
    bsz, seq, cols = f_rwkv.shape
    steps = ROW_TILE // bsz
    batch_major = pl.BlockSpec((bsz, steps, RWKV_DIM), lambda i: (0, i, 0))
    return pl.pallas_call(
        _prep_kernel,
        grid=(seq // steps,),
        in_specs=[pl.BlockSpec((bsz, steps, cols), lambda i: (0, i, 0)),
                  pl.BlockSpec((bsz, SUBLANES, cols),
                               lambda i: (0, jnp.maximum(i * (steps // SUBLANES) - 1, 0), 0)),
                  _resident(mu.shape), _resident(lora_w.shape), _resident(g_up.shape),
                  _resident(vecs.shape), _resident(ones_bd.shape), _resident(perms.shape)],
        out_specs=[pl.BlockSpec((SCAN_INPUTS, ROW_TILE, RWKV_DIM), lambda i: (0, i, 0)),
                   batch_major, batch_major],
        out_shape=[jax.ShapeDtypeStruct((SCAN_INPUTS, seq * bsz, RWKV_DIM), F32)]
        + [jax.ShapeDtypeStruct((bsz, seq, RWKV_DIM), F32)] * 2,
        scratch_shapes=[pltpu.VMEM((bsz, steps // SUBLANES, SUBLANES, RWKV_DIM), F32)],
        compiler_params=_params("arbitrary"),
        name="rwkv_prep",
    )(f_rwkv, f_rwkv, mu, lora_w, g_up, vecs, ones_bd, perms)


def _scan_kernel(x_ref, y_ref, state_ref, rows_even_ref, rows_odd_ref, yraw_ref):
    pairs = state_ref.shape[-1]
    bsz = x_ref.shape[2]
    two_heads = 2 * HEAD_DIM
    head_pairs = [slice(hp * two_heads, (hp + 1) * two_heads) for hp in range(RWKV_HEADS // 2)]
    R, W, K, V, KK, B = range(SCAN_INPUTS)
    per_group = HEAD_DIM // UPDATE_GROUPS

    def transpose_in(i, t, rows_ref):
        xt = jnp.concatenate([x_ref[i, t, :, lanes] for lanes in head_pairs], axis=0).T
        xt = jnp.concatenate([xt[:HEAD_DIM], xt[HEAD_DIM:]], axis=1)
        rows_ref[i] = xt.reshape(SUBLANES, SUBLANES, pairs)

    def write_out(t):
        y = yraw_ref[...]
        cen = y - jnp.mean(y, axis=0, keepdims=True)
        var = jnp.mean(cen * cen, axis=0, keepdims=True)
        yn = cen * jax.lax.rsqrt(var + GN_EPS)
        slab = jnp.concatenate([yn[:, :pairs // 2], yn[:, pairs // 2:]], axis=0).T
        for hp, lanes in enumerate(head_pairs):
            y_ref[t, :, lanes] = slab[hp * bsz:(hp + 1) * bsz]

    @pl.when(pl.program_id(0) == 0)
    def _():
        state_ref[...] = jnp.zeros_like(state_ref)
        yraw_ref[...] = jnp.zeros_like(yraw_ref)

    for i in range(SCAN_INPUTS):
        transpose_in(i, 0, rows_even_ref)

    def step(t, rows_ref, next_rows_ref):
        t_next = jnp.minimum(t + 1, SCAN_STEPS - 1)
        write_out(jnp.maximum(t - 1, 0))

        acc = jnp.zeros((HEAD_DIM, pairs), F32)
        for j in range(HEAD_DIM):
            acc = acc + state_ref[j] * rows_ref[KK, j // SUBLANES, j % SUBLANES:j % SUBLANES + 1, :]
        sa = -acc
        v = rows_ref[V].reshape(HEAD_DIM, pairs)

        def update(grp, y):
            for u in range(SCAN_INPUTS // UPDATE_GROUPS):
                transpose_in(grp * (SCAN_INPUTS // UPDATE_GROUPS) + u, t_next, next_rows_ref)
            for jj in range(per_group):
                j = grp * per_group + jj
                jh = grp * (per_group // SUBLANES) + jj // SUBLANES
                kl = slice(jj % SUBLANES, jj % SUBLANES + 1)
                s_new = (state_ref[j] * rows_ref[W, jh, kl, :] + sa * rows_ref[B, jh, kl, :]
                         + v * rows_ref[K, jh, kl, :])
                state_ref[j] = s_new
                y = y + s_new * rows_ref[R, jh, kl, :]
            return y

        yraw_ref[...] = jax.lax.fori_loop(0, UPDATE_GROUPS, update, jnp.zeros((HEAD_DIM, pairs), F32))

    def two_steps(i, carry):
        step(2 * i, rows_even_ref, rows_odd_ref)
        step(2 * i + 1, rows_odd_ref, rows_even_ref)
        return carry

    jax.lax.fori_loop(0, SCAN_STEPS // 2, two_steps, 0)
    write_out(SCAN_STEPS - 1)


def _scan(x):
    _, seq, bsz, _ = x.shape
    pairs = bsz * RWKV_HEADS
    return pl.pallas_call(
        _scan_kernel,
        grid=(seq // SCAN_STEPS,),
        in_specs=[pl.BlockSpec((SCAN_INPUTS, SCAN_STEPS, bsz, RWKV_DIM), lambda i: (0, i, 0, 0))],
        out_specs=pl.BlockSpec((SCAN_STEPS, bsz, RWKV_DIM), lambda i: (i, 0, 0)),
        out_shape=jax.ShapeDtypeStruct((seq, bsz, RWKV_DIM), F32),
        scratch_shapes=[pltpu.VMEM((HEAD_DIM, HEAD_DIM, pairs), F32),
                        pltpu.VMEM((SCAN_INPUTS, SUBLANES, SUBLANES, pairs), F32),
                        pltpu.VMEM((SCAN_INPUTS, SUBLANES, SUBLANES, pairs), F32),
                        pltpu.VMEM((HEAD_DIM, pairs), F32)],
        compiler_params=_params("arbitrary"),
        name="rwkv_scan",
    )(x)


def _attn_kernel(q_ref, kv_ref, kvp_ref, bias_ref, sink_ref, o_ref):
    kv = jnp.concatenate([kvp_ref[...], kv_ref[...]], axis=0)
    low = jax.lax.broadcasted_iota(jnp.int32, (2 * ATTN_BLOCK, 2 * HEAD_DIM), 1) < HEAD_DIM
    low_out = jax.lax.broadcasted_iota(jnp.int32, (ATTN_BLOCK, 2 * HEAD_DIM), 1) < HEAD_DIM
    nothing = jnp.zeros((2 * ATTN_BLOCK, 2 * HEAD_DIM), BF16)

    def block_diag(t):
        return jnp.concatenate([jnp.where(low, t, nothing), jnp.where(low, nothing, t)], axis=0)

    pairs = range(ATTN_HEADS // 2)
    pairs_per_kv = ATTN_GROUP // 2
    lanes = [slice(n * 2 * HEAD_DIM, (n + 1) * 2 * HEAD_DIM) for n in range(2 * ATTN_KV_HEADS)]
    k_pairs = [block_diag(kv[:, lanes[kvh]]) for kvh in range(ATTN_KV_HEADS)]
    v_pairs = [block_diag(kv[:, lanes[ATTN_KV_HEADS + kvh]]) for kvh in range(ATTN_KV_HEADS)]
    scores = []
    for pair in pairs:
        q2 = q_ref[:, lanes[pair]] * (HEAD_DIM ** -0.5)
        s = jax.lax.dot_general(q2, k_pairs[pair // pairs_per_kv], (((1,), (1,)), ((), ())),
                                preferred_element_type=F32)
        scores.append(s + bias_ref[0, pair])
    probs, scales = [], []
    for pair in pairs:
        halves, inv = [], []
        for half in range(2):
            sh = scores[pair][:, half * 2 * ATTN_BLOCK:(half + 1) * 2 * ATTN_BLOCK]
            sink = sink_ref[2 * pair + half]
            m = jnp.maximum(jnp.max(sh, axis=-1, keepdims=True), sink)
            p = jnp.exp(sh - m)
            inv.append(1.0 / (jnp.sum(p, axis=-1, keepdims=True) + jnp.exp(sink - m)))
            halves.append(p.astype(BF16))
        probs.append(jnp.concatenate(halves, axis=1))
        scales.append(jnp.where(low_out, inv[0], inv[1]))
    for pair in pairs:
        o = jnp.dot(probs[pair], v_pairs[pair // pairs_per_kv], preferred_element_type=F32)
        o_ref[:, lanes[pair]] = (o * scales[pair]).astype(o_ref.dtype)


def _attn_bias():
    qi = jnp.arange(ATTN_BLOCK)[:, None]
    sj = jnp.arange(2 * ATTN_BLOCK)[None, :]
    dist = qi + ATTN_BLOCK - sj
    in_window = (dist >= 0) & (dist < ATTN_BLOCK)
    slopes = 2.0 ** (-8.0 * jnp.arange(1, ATTN_HEADS + 1, dtype=F32) / ATTN_HEADS)
    alibi = -slopes[:, None, None] * dist.astype(F32)
    valid = jnp.stack([in_window & (sj >= ATTN_BLOCK), in_window])[:, None]
    bias = jnp.where(valid, alibi[None], NEG_BIG)
    bias = bias.reshape(2, ATTN_HEADS // 2, 2, ATTN_BLOCK, 2 * ATTN_BLOCK).transpose(0, 1, 3, 2, 4)
    return bias.reshape(2, ATTN_HEADS // 2, ATTN_BLOCK, 4 * ATTN_BLOCK)


def _attn(q, kv, sinks, blocks_per_seq):
    rows = q.shape[0]
    bias = _attn_bias()
    return pl.pallas_call(
        _attn_kernel,
        grid=(rows // ATTN_BLOCK // blocks_per_seq, blocks_per_seq),
        in_specs=[pl.BlockSpec((ATTN_BLOCK, ATTN_DIM), lambda b, i: (b * blocks_per_seq + i, 0)),
                  pl.BlockSpec((ATTN_BLOCK, 4 * KV_DIM), lambda b, i: (b * blocks_per_seq + i, 0)),
                  pl.BlockSpec((ATTN_BLOCK, 4 * KV_DIM),
                               lambda b, i: (b * blocks_per_seq + jnp.maximum(i - 1, 0), 0)),
                  pl.BlockSpec((1,) + bias.shape[1:], lambda b, i: (jnp.minimum(i, 1), 0, 0, 0)),
                  pl.BlockSpec(memory_space=pltpu.SMEM)],
        out_specs=pl.BlockSpec((ATTN_BLOCK, ATTN_DIM), lambda b, i: (b * blocks_per_seq + i, 0)),
        out_shape=jax.ShapeDtypeStruct((rows, ATTN_DIM), BF16),
        compiler_params=_params("arbitrary", "arbitrary"),
        name="swa_attn",
    )(q, kv, kv, bias, sinks)


def _merge_kernel(x_ref, mod_ref, yn_ref, bonus_ref, g_ref, yb_ref, ga_ref, gb_ref, gn_ref, n2_ref,
                  wa_ref, wb_ref, wo_ref, perm_ref, h_ref, u_ref, group_ref):
    bsz, steps, _ = x_ref.shape
    group_rows = bsz * SUBLANES
    to_batch = perm_ref[1]
    for grp in range(steps // SUBLANES):
        rows = yn_ref[grp * SUBLANES:(grp + 1) * SUBLANES].reshape(group_rows, RWKV_DIM)
        group_ref[:, grp] = _permute_rows(to_batch, rows).reshape(bsz, SUBLANES, RWKV_DIM)
    yn = group_ref[...].reshape(ROW_TILE, RWKV_DIM)

    def tile(ref):
        return ref[...].reshape(ROW_TILE, ref.shape[-1])

    ya = ((yn * gn_ref[0:1, :] + gn_ref[1:2, :]) + tile(bonus_ref)) * tile(g_ref)
    pa = _bdot(ya, wa_ref[...])
    pb = jnp.dot(tile(yb_ref), wb_ref[...], preferred_element_type=F32)
    merged = (jax.nn.sigmoid(tile(ga_ref).astype(F32)) * pa
              + jax.nn.sigmoid(tile(gb_ref).astype(F32)) * pb)
    h = tile(x_ref) + _per_batch_rows(mod_ref[2], bsz) * _bdot(merged, wo_ref[...])
    h_ref[...] = h.reshape(h_ref.shape)
    u = _rms_mod(h, n2_ref[...], _per_batch_rows(mod_ref[4], bsz), _per_batch_rows(mod_ref[3], bsz))
    u_ref[...] = u.astype(BF16).reshape(u_ref.shape)


def _merge(x3, mod_rows, yn, bonus, g, yb, ga, gb, gn, norm2_g, wa, wb, wo, perms):
    bsz, seq, _ = x3.shape
    steps = ROW_TILE // bsz

    def tile(width):
        return pl.BlockSpec((bsz, steps, width), lambda i: (0, i, 0))

    return pl.pallas_call(
        _merge_kernel,
        grid=(seq // steps,),
        in_specs=[tile(D_MODEL), _resident(mod_rows.shape),
                  pl.BlockSpec((steps, bsz, RWKV_DIM), lambda i: (i, 0, 0)),
                  tile(RWKV_DIM), tile(RWKV_DIM), tile(ATTN_DIM), tile(D_MODEL), tile(D_MODEL),
                  _resident(gn.shape), _resident((1, D_MODEL)),
                  _resident(wa.shape), _resident(wb.shape), _resident(wo.shape), _resident(perms.shape)],
        out_specs=[tile(D_MODEL), tile(D_MODEL)],
        out_shape=[jax.ShapeDtypeStruct((bsz, seq, D_MODEL), F32),
                   jax.ShapeDtypeStruct((bsz, seq, D_MODEL), BF16)],
        scratch_shapes=[pltpu.VMEM((bsz, steps // SUBLANES, SUBLANES, RWKV_DIM), F32)],
        compiler_params=_params("arbitrary"),
        name="merge",
    )(x3, mod_rows, yn, bonus, g, yb, ga, gb, gn, norm2_g.reshape(1, D_MODEL), wa, wb, wo, perms)


def _ffn_kernel(tiles_per_seq, h_ref, u_ref, mod_ref, wg_ref, wv_ref, cw_ref, cb_ref, wd_ref, fg_ref,
                o_ref, carry_ref, act_ref):
    @pl.when(pl.program_id(0) % tiles_per_seq == 0)
    def _():
        carry_ref[...] = jnp.zeros_like(carry_ref)

    u = u_ref[...]
    row = jax.lax.broadcasted_iota(jnp.int32, (ROW_TILE, FFN_CHUNK), 0)
    chunks = [slice(c * FFN_CHUNK, (c + 1) * FFN_CHUNK) for c in range(D_FF // FFN_CHUNK)]

    def up(cols):
        return (jnp.dot(u, wg_ref[:, cols], preferred_element_type=F32),
                jnp.dot(u, wv_ref[:, cols], preferred_element_type=F32))

    ahead = [up(cols) for cols in chunks[:FFN_LOOKAHEAD]]
    for c, cols in enumerate(chunks):
        gate, val = ahead.pop(0)
        if c + FFN_LOOKAHEAD < len(chunks):
            ahead.append(up(chunks[c + FFN_LOOKAHEAD]))
        back1 = carry_ref[SUBLANES - 1:SUBLANES, cols]
        back2 = carry_ref[SUBLANES - 2:SUBLANES - 1, cols]
        carry_ref[:, cols] = gate[ROW_TILE - SUBLANES:, :]
        shift1 = jnp.where(row == 0, back1, pltpu.roll(gate, 1, 0))
        shift2 = jnp.where(row == 0, back2, jnp.where(row == 1, back1, pltpu.roll(gate, 2, 0)))
        conv = (cw_ref[0:1, cols] * shift2 + cw_ref[1:2, cols] * shift1 + cw_ref[2:3, cols] * gate
                + cb_ref[:, cols])
        act_ref[:, cols] = (conv * jax.nn.sigmoid(conv) * val).astype(BF16)
    acc = jnp.dot(act_ref[...], wd_ref[...], preferred_element_type=F32)
    h = h_ref[...] + mod_ref[0, 5:6, :] * acc
    o_ref[...] = h * jax.lax.rsqrt(jnp.mean(h * h, axis=-1, keepdims=True) + RMS_EPS) * fg_ref[...]


def _ffn(h1, u2, mod3, w_gate, w_val, conv_w, conv_b, w_down, final_g, tiles_per_seq):
    rows = h1.shape[0]
    return pl.pallas_call(
        functools.partial(_ffn_kernel, tiles_per_seq),
        grid=(rows // ROW_TILE,),
        in_specs=[pl.BlockSpec((ROW_TILE, D_MODEL), lambda i: (i, 0)),
                  pl.BlockSpec((ROW_TILE, D_MODEL), lambda i: (i, 0)),
                  pl.BlockSpec((1, 6, D_MODEL), lambda i: (i // tiles_per_seq, 0, 0)),
                  _resident(w_gate.shape), _resident(w_val.shape), _resident(conv_w.shape),
                  _resident((1, D_FF)), _resident(w_down.shape), _resident((1, D_MODEL))],
        out_specs=pl.BlockSpec((ROW_TILE, D_MODEL), lambda i: (i, 0)),
        out_shape=jax.ShapeDtypeStruct((rows, D_MODEL), F32),
        scratch_shapes=[pltpu.VMEM((SUBLANES, D_FF), F32), pltpu.VMEM((ROW_TILE, D_FF), BF16)],
        compiler_params=_params("arbitrary"),
        name="ffn",
    )(h1, u2, mod3, w_gate, w_val, conv_w, conv_b.reshape(1, D_FF), w_down, final_g.reshape(1, D_MODEL))


def kernel(x, c, ada_w, ada_b, norm1_g, w_in, rwkv_mu, rwkv_w0, rwkv_w_up, rwkv_a0, rwkv_a_up, rwkv_g_up, rwkv_k_k, rwkv_k_a, rwkv_r_k, rwkv_gn_w, rwkv_gn_b, attn_sinks, w_branch_a, w_branch_b, w_out, norm2_g, ffn_w_up, ffn_conv_w, ffn_conv_b, ffn_w_down, final_g):
    bsz, seq, _ = x.shape
    depth = ada_w.shape[0]
    rows = bsz * seq
    assert depth == 1, "the ffn kernel fuses the final norm, which follows the last layer only"
    assert bsz * RWKV_HEADS == 128 and seq % ROW_TILE == 0 and seq % SCAN_STEPS == 0
    tiles_per_seq = seq // ROW_TILE
    ones_bd = jnp.kron(jnp.eye(RWKV_HEADS, dtype=F32), jnp.ones((HEAD_DIM, HEAD_DIM), F32)).astype(BF16)
    perms = _row_order_perms(bsz)
    h = x.reshape(rows, D_MODEL)
    for l in range(depth):
        mod3 = _mod(c, ada_w[l], ada_b[l]).reshape(bsz, 6, D_MODEL)
        mod_rows = jnp.broadcast_to(mod3.transpose(1, 0, 2)[:, :, None, :], (6, bsz, SUBLANES, D_MODEL))
        kv0 = RWKV_COLS + ATTN_DIM
        kv_cols = kv0 + (jnp.arange(4 * KV_DIM) // (2 * HEAD_DIM)) * HEAD_DIM + jnp.arange(4 * KV_DIM) % HEAD_DIM
        w_proj = jnp.concatenate([w_in[l][:, :kv0], w_in[l][:, kv_cols], w_in[l][:, kv0 + 2 * KV_DIM:]], axis=1)
        f_rwkv, q, kv, gate_a, gate_b = _in_proj(h, mod3, norm1_g[l], w_proj.astype(BF16), tiles_per_seq)

        lora_w = jnp.zeros((LORA_COLS, 2 * RWKV_DIM), F32)
        lora_w = lora_w.at[:DECAY_LORA, :RWKV_DIM].set(rwkv_w_up[l]).at[DECAY_LORA:, RWKV_DIM:].set(rwkv_a_up[l])
        vecs = jnp.stack([rwkv_w0[l], rwkv_a0[l], rwkv_k_k[l], rwkv_k_a[l], rwkv_r_k[l].reshape(RWKV_DIM)])
        scan_in, g, bonus = _prep(f_rwkv.reshape(bsz, seq, RWKV_COLS), rwkv_mu[l].reshape(1, RWKV_COLS),
                                  lora_w, rwkv_g_up[l], vecs, ones_bd, perms)
        yn = _scan(scan_in.reshape(SCAN_INPUTS, seq, bsz, RWKV_DIM))

        yb = _attn(q, kv, attn_sinks[l], seq // ATTN_BLOCK)

        gn = jnp.stack([rwkv_gn_w[l], rwkv_gn_b[l]])
        by_batch = lambda t: t.reshape(bsz, seq, t.shape[-1])
        h3, u2 = _merge(by_batch(h), mod_rows, yn, bonus, g, by_batch(yb), by_batch(gate_a), by_batch(gate_b),
                        gn, norm2_g[l], w_branch_a[l].astype(BF16), w_branch_b[l].astype(BF16),
                        w_out[l].astype(BF16), perms)
        w_up = ffn_w_up[l].astype(BF16)
        h = _ffn(h3.reshape(rows, D_MODEL), u2.reshape(rows, D_MODEL), mod3, w_up[:, :D_FF], w_up[:, D_FF:],
                 ffn_conv_w[l], ffn_conv_b[l], ffn_w_down[l].astype(BF16), final_g, tiles_per_seq)
    return h.reshape(bsz, seq, D_MODEL)
```

```python
import functools

import jax
import jax.numpy as jnp
from jax.experimental import pallas as pl
from jax.experimental.pallas import tpu as pltpu

F32 = jnp.float32
BF16 = jnp.bfloat16

D_MODEL = 1024
RWKV_HEADS = 8
HEAD_DIM = 64
RWKV_DIM = RWKV_HEADS * HEAD_DIM
DECAY_LORA = 64
AAA_LORA = 64
GATE_LORA = 128
LORA_COLS = DECAY_LORA + AAA_LORA
RWKV_COLS = 3 * RWKV_DIM + LORA_COLS + GATE_LORA
ATTN_HEADS = 8
ATTN_KV_HEADS = 2
ATTN_GROUP = ATTN_HEADS // ATTN_KV_HEADS
ATTN_DIM = ATTN_HEADS * HEAD_DIM
KV_DIM = ATTN_KV_HEADS * HEAD_DIM
ATTN_BLOCK = 128
D_FF = 2816
GN_EPS = 64e-5
RMS_EPS = 1e-6
DECAY_SCALE = 0.6065306597126334

VMEM_LIMIT_BYTES = 56 * 1024 * 1024
ROW_TILE = 512
FFN_CHUNK = 256
FFN_LOOKAHEAD = 2
SCAN_STEPS = 32
SCAN_INPUTS = 5
UPDATE_GROUPS = 2
SUBLANES = 8
NEG_BIG = -1e30


def _params(*semantics):
    return pltpu.CompilerParams(dimension_semantics=semantics, vmem_limit_bytes=VMEM_LIMIT_BYTES)


def _resident(shape):
    zeros = (0,) * len(shape)
    return pl.BlockSpec(shape, lambda *_: zeros, pipeline_mode=pl.Buffered(1))


def _bdot(a, b):
    return jnp.dot(a.astype(BF16), b.astype(BF16), preferred_element_type=F32)


def _split(x):
    hi = x.astype(BF16)
    lo = (x - hi.astype(F32)).astype(BF16)
    return hi, lo


def _dot3(a, b):
    a_hi, a_lo = _split(a)
    b_hi, b_lo = _split(b)
    dot = functools.partial(jnp.dot, preferred_element_type=F32)
    return dot(a_hi, b_hi) + dot(a_lo, b_hi) + dot(a_hi, b_lo)


def _head_sum(x, ones_blockdiag):
    hi, lo = _split(x)
    dot = functools.partial(jnp.dot, preferred_element_type=F32)
    return dot(hi, ones_blockdiag) + dot(lo, ones_blockdiag)


def _rms_mod(h, gain, scale, shift):
    y = h * jax.lax.rsqrt(jnp.mean(h * h, axis=-1, keepdims=True) + RMS_EPS)
    return y * gain * (1.0 + scale) + shift


def _permute_rows(p, x, pieces=3):
    dot = functools.partial(jnp.dot, preferred_element_type=F32)
    out = None
    for _ in range(pieces):
        piece = x.astype(BF16)
        x = x - piece.astype(F32)
        out = dot(p, piece) if out is None else out + dot(p, piece)
    return out


def _row_order_perms(bsz):
    rows = bsz * SUBLANES
    src = jnp.arange(rows)[None, :]
    to_time = jnp.arange(rows)[:, None] == (src % SUBLANES) * bsz + src // SUBLANES
    return jnp.stack([to_time, to_time.T]).astype(BF16)


def _per_batch_rows(t, bsz):
    groups = ROW_TILE // bsz // SUBLANES
    return jnp.broadcast_to(t[:, None], (bsz, groups) + t.shape[1:]).reshape(ROW_TILE, t.shape[-1])


def _mod_kernel(c_ref, w_ref, b_ref, o_ref):
    c = c_ref[...]
    o_ref[...] = _bdot(c * jax.nn.sigmoid(c), w_ref[...]) + b_ref[...]


def _mod(c, ada_w, ada_b):
    bsz = c.shape[0]
    cols = ada_w.shape[1]
    tile = cols // 4
    return pl.pallas_call(
        _mod_kernel,
        grid=(cols // tile,),
        in_specs=[pl.BlockSpec((bsz, D_MODEL), lambda j: (0, 0)),
                  pl.BlockSpec((D_MODEL, tile), lambda j: (0, j)),
                  pl.BlockSpec((1, tile), lambda j: (0, j))],
        out_specs=pl.BlockSpec((bsz, tile), lambda j: (0, j)),
        out_shape=jax.ShapeDtypeStruct((bsz, cols), F32),
        compiler_params=_params("arbitrary"),
        name="mod",
    )(c, ada_w, ada_b.reshape(1, cols))


def _in_proj_kernel(x_ref, mod_ref, g_ref, w_ref, f_ref, q_ref, kv_ref, ga_ref, gb_ref):
    u = _rms_mod(x_ref[...], g_ref[...], mod_ref[0, 1:2, :], mod_ref[0, 0:1, :]).astype(BF16)
    col = 0
    for out in (f_ref, q_ref, kv_ref, ga_ref, gb_ref):
        width = out.shape[1]
        out[...] = jnp.dot(u, w_ref[:, col:col + width], preferred_element_type=F32).astype(out.dtype)
        col += width


def _in_proj(x2, mod3, norm_g, w_in, tiles_per_seq):
    rows = x2.shape[0]
    widths = (RWKV_COLS, ATTN_DIM, 4 * KV_DIM, D_MODEL, D_MODEL)
    dtypes = (F32, BF16, BF16, BF16, BF16)
    return pl.pallas_call(
        _in_proj_kernel,
        grid=(rows // ROW_TILE,),
        in_specs=[pl.BlockSpec((ROW_TILE, D_MODEL), lambda i: (i, 0)),
                  pl.BlockSpec((1, 6, D_MODEL), lambda i: (i // tiles_per_seq, 0, 0)),
                  _resident((1, D_MODEL)),
                  _resident(w_in.shape)],
        out_specs=[pl.BlockSpec((ROW_TILE, w), lambda i: (i, 0)) for w in widths],
        out_shape=[jax.ShapeDtypeStruct((rows, w), d) for w, d in zip(widths, dtypes)],
        compiler_params=_params("arbitrary"),
        name="in_proj",
    )(x2, mod3, norm_g.reshape(1, D_MODEL), w_in)


def _prep_kernel(f_ref, prev_ref, mu_ref, lora_ref, gup_ref, vec_ref, ones_ref, perm_ref,
                 x_ref, pend_ref, g_ref, bonus_ref, group_ref):
    bsz, steps, cols = f_ref.shape
    feat = f_ref[...].reshape(ROW_TILE, cols)
    before = pltpu.roll(prev_ref[...].reshape(bsz * SUBLANES, cols), bsz * SUBLANES - (SUBLANES - 1), 0)
    before = jnp.where(pl.program_id(0) == 0, 0.0, before)
    before = _per_batch_rows(before.reshape(bsz, SUBLANES, cols), bsz)
    row = jax.lax.broadcasted_iota(jnp.int32, feat.shape, 0)
    shifted = jnp.where(row % steps == 0, before, pltpu.roll(feat, 1, 0))
    f = feat + (shifted - feat) * mu_ref[...]

    r = f[:, 0:RWKV_DIM]
    k = f[:, RWKV_DIM:2 * RWKV_DIM]
    v = f[:, 2 * RWKV_DIM:3 * RWKV_DIM]
    lo = f[:, 3 * RWKV_DIM:3 * RWKV_DIM + LORA_COLS]
    g_lo = f[:, 3 * RWKV_DIM + LORA_COLS:]
    w0, a0, k_k, k_a, r_k = (vec_ref[i:i + 1, :] for i in range(5))

    lane = jax.lax.broadcasted_iota(jnp.int32, lo.shape, 1)
    lora = _dot3(jnp.where(lane < DECAY_LORA, jnp.tanh(lo), lo), lora_ref[...])
    log_decay = -DECAY_SCALE * jax.nn.sigmoid(w0 + lora[:, :RWKV_DIM])
    step_in_tile = jax.lax.broadcasted_iota(jnp.int32, (ROW_TILE, RWKV_DIM), 0) % steps
    cum = log_decay
    shift = 1
    while shift < steps:
        cum = cum + jnp.where(step_in_tile >= shift, pltpu.roll(cum, shift, 0), 0.0)
        shift *= 2
    decayed = jnp.exp(cum)
    grown = jnp.exp(-cum)
    decayed_before = jnp.exp(cum - log_decay)
    a = jax.nn.sigmoid(a0 + lora[:, RWKV_DIM:])
    g = _bdot(jax.nn.sigmoid(g_lo), gup_ref[...])

    ones = ones_ref[...]
    kk = k * k_k
    kk = kk / jnp.maximum(jnp.sqrt(_head_sum(kk * kk, ones)), 1e-12)
    k = k * (1.0 + (a - 1.0) * k_a)

    to_time = perm_ref[0]
    group_rows = bsz * SUBLANES
    groups = steps // SUBLANES

    def time_major(grp, pieces):
        return _permute_rows(to_time, group_ref[:, grp].reshape(group_rows, RWKV_DIM), pieces)

    scan_inputs = (r * decayed, k * grown, v, kk * decayed_before, kk * a * grown)
    for i, t in enumerate(scan_inputs):
        group_ref[...] = t.reshape(bsz, groups, SUBLANES, RWKV_DIM)
        for grp in range(groups):
            x_ref[i, grp * group_rows:(grp + 1) * group_rows, :] = time_major(grp, 2)
    group_ref[...] = decayed.reshape(bsz, groups, SUBLANES, RWKV_DIM)
    pend_ref[...] = time_major(groups - 1, 3)[(SUBLANES - 1) * bsz:]
    g_ref[...] = g.astype(BF16).reshape(bsz, steps, RWKV_DIM)
    bonus_ref[...] = (_head_sum(r * k * r_k, ones) * v).astype(BF16).reshape(bsz, steps, RWKV_DIM)


def _prep(f_rwkv, mu, lora_w, g_up, vecs, ones_bd, perms):
    bsz, seq, cols = f_rwkv.shape
    steps = ROW_TILE // bsz
    batch_major = pl.BlockSpec((bsz, steps, RWKV_DIM), lambda i: (0, i, 0))
    return pl.pallas_call(
        _prep_kernel,
        grid=(seq // steps,),
        in_specs=[pl.BlockSpec((bsz, steps, cols), lambda i: (0, i, 0)),
                  pl.BlockSpec((bsz, SUBLANES, cols),
                               lambda i: (0, jnp.maximum(i * (steps // SUBLANES) - 1, 0), 0)),
                  _resident(mu.shape), _resident(lora_w.shape), _resident(g_up.shape),
                  _resident(vecs.shape), _resident(ones_bd.shape), _resident(perms.shape)],
        out_specs=[pl.BlockSpec((SCAN_INPUTS, ROW_TILE, RWKV_DIM), lambda i: (0, i, 0)),
                   pl.BlockSpec((bsz, RWKV_DIM), lambda i: (i, 0)),
                   batch_major, batch_major],
        out_shape=[jax.ShapeDtypeStruct((SCAN_INPUTS, seq * bsz, RWKV_DIM), F32),
                   jax.ShapeDtypeStruct((seq // steps * bsz, RWKV_DIM), F32)]
        + [jax.ShapeDtypeStruct((bsz, seq, RWKV_DIM), BF16)] * 2,
        scratch_shapes=[pltpu.VMEM((bsz, steps // SUBLANES, SUBLANES, RWKV_DIM), F32)],
        compiler_params=_params("arbitrary"),
        name="rwkv_prep",
    )(f_rwkv, f_rwkv, mu, lora_w, g_up, vecs, ones_bd, perms)


def _scan_kernel(x_ref, pend_ref, y_ref, state_ref, rows_even_ref, rows_odd_ref, yraw_ref):
    pairs = state_ref.shape[-1]
    bsz = x_ref.shape[2]
    two_heads = 2 * HEAD_DIM
    head_pairs = [slice(hp * two_heads, (hp + 1) * two_heads) for hp in range(RWKV_HEADS // 2)]
    R, K, V, KK, B = range(SCAN_INPUTS)
    per_group = HEAD_DIM // UPDATE_GROUPS
    in_update = (SCAN_INPUTS - 1) // UPDATE_GROUPS

    def regroup(slab_of):
        xt = jnp.concatenate([slab_of(lanes) for lanes in head_pairs], axis=0).T
        xt = jnp.concatenate([xt[:HEAD_DIM], xt[HEAD_DIM:]], axis=1)
        return xt.reshape(SUBLANES, SUBLANES, pairs)

    def transpose_in(i, t, rows_ref):
        rows_ref[i] = regroup(lambda lanes: x_ref[i, t, :, lanes])

    def write_out(t):
        y = yraw_ref[...]
        cen = y - jnp.mean(y, axis=0, keepdims=True)
        var = jnp.mean(cen * cen, axis=0, keepdims=True)
        yn = cen * jax.lax.rsqrt(var + GN_EPS)
        slab = jnp.concatenate([yn[:, :pairs // 2], yn[:, pairs // 2:]], axis=0).T
        for hp, lanes in enumerate(head_pairs):
            y_ref[t, :, lanes] = slab[hp * bsz:(hp + 1) * bsz]

    @pl.when(pl.program_id(0) == 0)
    def _():
        state_ref[...] = jnp.zeros_like(state_ref)
        yraw_ref[...] = jnp.zeros_like(yraw_ref)

    pend = regroup(lambda lanes: pend_ref[:, lanes]).reshape(HEAD_DIM, pairs)
    for j in range(HEAD_DIM):
        state_ref[j] = state_ref[j] * pend[j:j + 1, :]

    for i in range(SCAN_INPUTS):
        transpose_in(i, 0, rows_even_ref)

    def step(t, rows_ref, next_rows_ref):
        t_next = jnp.minimum(t + 1, SCAN_STEPS - 1)
        write_out(jnp.maximum(t - 1, 0))
        for i in range(in_update * UPDATE_GROUPS, SCAN_INPUTS):
            transpose_in(i, t_next, next_rows_ref)

        acc = jnp.zeros((HEAD_DIM, pairs), F32)
        for j in range(HEAD_DIM):
            acc = acc + state_ref[j] * rows_ref[KK, j // SUBLANES, j % SUBLANES:j % SUBLANES + 1, :]
        sa = -acc
        v = rows_ref[V].reshape(HEAD_DIM, pairs)

        def update(grp, y):
            for u in range(in_update):
                transpose_in(grp * in_update + u, t_next, next_rows_ref)
            for jj in range(per_group):
                j = grp * per_group + jj
                jh = grp * (per_group // SUBLANES) + jj // SUBLANES
                kl = slice(jj % SUBLANES, jj % SUBLANES + 1)
                s_new = state_ref[j] + sa * rows_ref[B, jh, kl, :] + v * rows_ref[K, jh, kl, :]
                state_ref[j] = s_new
                y = y + s_new * rows_ref[R, jh, kl, :]
            return y

        yraw_ref[...] = jax.lax.fori_loop(0, UPDATE_GROUPS, update, jnp.zeros((HEAD_DIM, pairs), F32))

    def two_steps(i, carry):
        step(2 * i, rows_even_ref, rows_odd_ref)
        step(2 * i + 1, rows_odd_ref, rows_even_ref)
        return carry

    jax.lax.fori_loop(0, SCAN_STEPS // 2, two_steps, 0)
    write_out(SCAN_STEPS - 1)


def _scan(x, pend):
    _, seq, bsz, _ = x.shape
    pairs = bsz * RWKV_HEADS
    return pl.pallas_call(
        _scan_kernel,
        grid=(seq // SCAN_STEPS,),
        in_specs=[pl.BlockSpec((SCAN_INPUTS, SCAN_STEPS, bsz, RWKV_DIM), lambda i: (0, i, 0, 0)),
                  pl.BlockSpec((bsz, RWKV_DIM), lambda i: (jnp.maximum(i - 1, 0), 0))],
        out_specs=pl.BlockSpec((SCAN_STEPS, bsz, RWKV_DIM), lambda i: (i, 0, 0)),
        out_shape=jax.ShapeDtypeStruct((seq, bsz, RWKV_DIM), F32),
        scratch_shapes=[pltpu.VMEM((HEAD_DIM, HEAD_DIM, pairs), F32),
                        pltpu.VMEM((SCAN_INPUTS, SUBLANES, SUBLANES, pairs), F32),
                        pltpu.VMEM((SCAN_INPUTS, SUBLANES, SUBLANES, pairs), F32),
                        pltpu.VMEM((HEAD_DIM, pairs), F32)],
        compiler_params=_params("arbitrary"),
        name="rwkv_scan",
    )(x, pend)


def _attn_kernel(q_ref, kv_ref, kvp_ref, bias_ref, sink_ref, o_ref):
    kv = jnp.concatenate([kvp_ref[...], kv_ref[...]], axis=0)
    low = jax.lax.broadcasted_iota(jnp.int32, (2 * ATTN_BLOCK, 2 * HEAD_DIM), 1) < HEAD_DIM
    low_out = jax.lax.broadcasted_iota(jnp.int32, (ATTN_BLOCK, 2 * HEAD_DIM), 1) < HEAD_DIM
    nothing = jnp.zeros((2 * ATTN_BLOCK, 2 * HEAD_DIM), BF16)

    def block_diag(t):
        return jnp.concatenate([jnp.where(low, t, nothing), jnp.where(low, nothing, t)], axis=0)

    pairs = range(ATTN_HEADS // 2)
    pairs_per_kv = ATTN_GROUP // 2
    lanes = [slice(n * 2 * HEAD_DIM, (n + 1) * 2 * HEAD_DIM) for n in range(2 * ATTN_KV_HEADS)]
    k_pairs = [block_diag(kv[:, lanes[kvh]]) for kvh in range(ATTN_KV_HEADS)]
    v_pairs = [block_diag(kv[:, lanes[ATTN_KV_HEADS + kvh]]) for kvh in range(ATTN_KV_HEADS)]
    scores = []
    for pair in pairs:
        q2 = q_ref[:, lanes[pair]] * (HEAD_DIM ** -0.5)
        s = jax.lax.dot_general(q2, k_pairs[pair // pairs_per_kv], (((1,), (1,)), ((), ())),
                                preferred_element_type=F32)
        scores.append(s + bias_ref[0, pair])
    probs, scales = [], []
    for pair in pairs:
        halves, inv = [], []
        for half in range(2):
            sh = scores[pair][:, half * 2 * ATTN_BLOCK:(half + 1) * 2 * ATTN_BLOCK]
            sink = sink_ref[2 * pair + half]
            m = jnp.maximum(jnp.max(sh, axis=-1, keepdims=True), sink)
            p = jnp.exp(sh - m)
            inv.append(1.0 / (jnp.sum(p, axis=-1, keepdims=True) + jnp.exp(sink - m)))
            halves.append(p.astype(BF16))
        probs.append(jnp.concatenate(halves, axis=1))
        scales.append(jnp.where(low_out, inv[0], inv[1]))
    for pair in pairs:
        o = jnp.dot(probs[pair], v_pairs[pair // pairs_per_kv], preferred_element_type=F32)
        o_ref[:, lanes[pair]] = (o * scales[pair]).astype(o_ref.dtype)


def _attn_bias():
    qi = jnp.arange(ATTN_BLOCK)[:, None]
    sj = jnp.arange(2 * ATTN_BLOCK)[None, :]
    dist = qi + ATTN_BLOCK - sj
    in_window = (dist >= 0) & (dist < ATTN_BLOCK)
    slopes = 2.0 ** (-8.0 * jnp.arange(1, ATTN_HEADS + 1, dtype=F32) / ATTN_HEADS)
    alibi = -slopes[:, None, None] * dist.astype(F32)
    valid = jnp.stack([in_window & (sj >= ATTN_BLOCK), in_window])[:, None]
    bias = jnp.where(valid, alibi[None], NEG_BIG)
    bias = bias.reshape(2, ATTN_HEADS // 2, 2, ATTN_BLOCK, 2 * ATTN_BLOCK).transpose(0, 1, 3, 2, 4)
    return bias.reshape(2, ATTN_HEADS // 2, ATTN_BLOCK, 4 * ATTN_BLOCK)


def _attn(q, kv, sinks, blocks_per_seq):
    rows = q.shape[0]
    bias = _attn_bias()
    return pl.pallas_call(
        _attn_kernel,
        grid=(rows // ATTN_BLOCK // blocks_per_seq, blocks_per_seq),
        in_specs=[pl.BlockSpec((ATTN_BLOCK, ATTN_DIM), lambda b, i: (b * blocks_per_seq + i, 0)),
                  pl.BlockSpec((ATTN_BLOCK, 4 * KV_DIM), lambda b, i: (b * blocks_per_seq + i, 0)),
                  pl.BlockSpec((ATTN_BLOCK, 4 * KV_DIM),
                               lambda b, i: (b * blocks_per_seq + jnp.maximum(i - 1, 0), 0)),
                  pl.BlockSpec((1,) + bias.shape[1:], lambda b, i: (jnp.minimum(i, 1), 0, 0, 0)),
                  pl.BlockSpec(memory_space=pltpu.SMEM)],
        out_specs=pl.BlockSpec((ATTN_BLOCK, ATTN_DIM), lambda b, i: (b * blocks_per_seq + i, 0)),
        out_shape=jax.ShapeDtypeStruct((rows, ATTN_DIM), BF16),
        compiler_params=_params("arbitrary", "arbitrary"),
        name="swa_attn",
    )(q, kv, kv, bias, sinks)


def _merge_kernel(x_ref, mod_ref, yn_ref, bonus_ref, g_ref, yb_ref, ga_ref, gb_ref, gn_ref, n2_ref,
                  wa_ref, wb_ref, wo_ref, perm_ref, h_ref, u_ref, group_ref):
    bsz, steps, _ = x_ref.shape
    group_rows = bsz * SUBLANES
    to_batch = perm_ref[1]
    for grp in range(steps // SUBLANES):
        rows = yn_ref[grp * SUBLANES:(grp + 1) * SUBLANES].reshape(group_rows, RWKV_DIM)
        group_ref[:, grp] = _permute_rows(to_batch, rows).reshape(bsz, SUBLANES, RWKV_DIM)
    yn = group_ref[...].reshape(ROW_TILE, RWKV_DIM)

    def tile(ref):
        return ref[...].reshape(ROW_TILE, ref.shape[-1])

    ya = ((yn * gn_ref[0:1, :] + gn_ref[1:2, :]) + tile(bonus_ref).astype(F32)) * tile(g_ref).astype(F32)
    pa = _bdot(ya, wa_ref[...])
    pb = jnp.dot(tile(yb_ref), wb_ref[...], preferred_element_type=F32)
    merged = (jax.nn.sigmoid(tile(ga_ref).astype(F32)) * pa
              + jax.nn.sigmoid(tile(gb_ref).astype(F32)) * pb)
    h = tile(x_ref) + _per_batch_rows(mod_ref[2], bsz) * _bdot(merged, wo_ref[...])
    h_ref[...] = h.reshape(h_ref.shape)
    u = _rms_mod(h, n2_ref[...], _per_batch_rows(mod_ref[4], bsz), _per_batch_rows(mod_ref[3], bsz))
    u_ref[...] = u.astype(BF16).reshape(u_ref.shape)


def _merge(x3, mod_rows, yn, bonus, g, yb, ga, gb, gn, norm2_g, wa, wb, wo, perms):
    bsz, seq, _ = x3.shape
    steps = ROW_TILE // bsz

    def tile(width):
        return pl.BlockSpec((bsz, steps, width), lambda i: (0, i, 0))

    return pl.pallas_call(
        _merge_kernel,
        grid=(seq // steps,),
        in_specs=[tile(D_MODEL), _resident(mod_rows.shape),
                  pl.BlockSpec((steps, bsz, RWKV_DIM), lambda i: (i, 0, 0)),
                  tile(RWKV_DIM), tile(RWKV_DIM), tile(ATTN_DIM), tile(D_MODEL), tile(D_MODEL),
                  _resident(gn.shape), _resident((1, D_MODEL)),
                  _resident(wa.shape), _resident(wb.shape), _resident(wo.shape), _resident(perms.shape)],
        out_specs=[tile(D_MODEL), tile(D_MODEL)],
        out_shape=[jax.ShapeDtypeStruct((bsz, seq, D_MODEL), F32),
                   jax.ShapeDtypeStruct((bsz, seq, D_MODEL), BF16)],
        scratch_shapes=[pltpu.VMEM((bsz, steps // SUBLANES, SUBLANES, RWKV_DIM), F32)],
        compiler_params=_params("arbitrary"),
        name="merge",
    )(x3, mod_rows, yn, bonus, g, yb, ga, gb, gn, norm2_g.reshape(1, D_MODEL), wa, wb, wo, perms)


def _ffn_kernel(tiles_per_seq, h_ref, u_ref, mod_ref, wg_ref, wv_ref, cw_ref, cb_ref, wd_ref, fg_ref,
                o_ref, carry_ref, act_ref):
    @pl.when(pl.program_id(0) % tiles_per_seq == 0)
    def _():
        carry_ref[...] = jnp.zeros_like(carry_ref)

    u = u_ref[...]
    row = jax.lax.broadcasted_iota(jnp.int32, (ROW_TILE, FFN_CHUNK), 0)
    chunks = [slice(c * FFN_CHUNK, (c + 1) * FFN_CHUNK) for c in range(D_FF // FFN_CHUNK)]

    def up(cols):
        return (jnp.dot(u, wg_ref[:, cols], preferred_element_type=F32),
                jnp.dot(u, wv_ref[:, cols], preferred_element_type=F32))

    ahead = [up(cols) for cols in chunks[:FFN_LOOKAHEAD]]
    for c, cols in enumerate(chunks):
        gate, val = ahead.pop(0)
        if c + FFN_LOOKAHEAD < len(chunks):
            ahead.append(up(chunks[c + FFN_LOOKAHEAD]))
        back1 = carry_ref[SUBLANES - 1:SUBLANES, cols]
        back2 = carry_ref[SUBLANES - 2:SUBLANES - 1, cols]
        carry_ref[:, cols] = gate[ROW_TILE - SUBLANES:, :]
        shift1 = jnp.where(row == 0, back1, pltpu.roll(gate, 1, 0))
        shift2 = jnp.where(row == 0, back2, jnp.where(row == 1, back1, pltpu.roll(gate, 2, 0)))
        conv = (cw_ref[0:1, cols] * shift2 + cw_ref[1:2, cols] * shift1 + cw_ref[2:3, cols] * gate
                + cb_ref[:, cols])
        act_ref[:, cols] = (conv * jax.nn.sigmoid(conv) * val).astype(BF16)
    acc = jnp.dot(act_ref[...], wd_ref[...], preferred_element_type=F32)
    h = h_ref[...] + mod_ref[0, 5:6, :] * acc
    o_ref[...] = h * jax.lax.rsqrt(jnp.mean(h * h, axis=-1, keepdims=True) + RMS_EPS) * fg_ref[...]


def _ffn(h1, u2, mod3, w_gate, w_val, conv_w, conv_b, w_down, final_g, tiles_per_seq):
    rows = h1.shape[0]
    return pl.pallas_call(
        functools.partial(_ffn_kernel, tiles_per_seq),
        grid=(rows // ROW_TILE,),
        in_specs=[pl.BlockSpec((ROW_TILE, D_MODEL), lambda i: (i, 0)),
                  pl.BlockSpec((ROW_TILE, D_MODEL), lambda i: (i, 0)),
                  pl.BlockSpec((1, 6, D_MODEL), lambda i: (i // tiles_per_seq, 0, 0)),
                  _resident(w_gate.shape), _resident(w_val.shape), _resident(conv_w.shape),
                  _resident((1, D_FF)), _resident(w_down.shape), _resident((1, D_MODEL))],
        out_specs=pl.BlockSpec((ROW_TILE, D_MODEL), lambda i: (i, 0)),
        out_shape=jax.ShapeDtypeStruct((rows, D_MODEL), F32),
        scratch_shapes=[pltpu.VMEM((SUBLANES, D_FF), F32), pltpu.VMEM((ROW_TILE, D_FF), BF16)],
        compiler_params=_params("arbitrary"),
        name="ffn",
    )(h1, u2, mod3, w_gate, w_val, conv_w, conv_b.reshape(1, D_FF), w_down, final_g.reshape(1, D_MODEL))


def kernel(x, c, ada_w, ada_b, norm1_g, w_in, rwkv_mu, rwkv_w0, rwkv_w_up, rwkv_a0, rwkv_a_up, rwkv_g_up, rwkv_k_k, rwkv_k_a, rwkv_r_k, rwkv_gn_w, rwkv_gn_b, attn_sinks, w_branch_a, w_branch_b, w_out, norm2_g, ffn_w_up, ffn_conv_w, ffn_conv_b, ffn_w_down, final_g):
    bsz, seq, _ = x.shape
    depth = ada_w.shape[0]
    rows = bsz * seq
    assert depth == 1, "the ffn kernel fuses the final norm, which follows the last layer only"
    assert bsz * RWKV_HEADS == 128 and seq % ROW_TILE == 0 and seq % SCAN_STEPS == 0
    assert ROW_TILE == bsz * SCAN_STEPS, "prep restarts the decay product per tile, the scan per block"
    tiles_per_seq = seq // ROW_TILE
    ones_bd = jnp.kron(jnp.eye(RWKV_HEADS, dtype=F32), jnp.ones((HEAD_DIM, HEAD_DIM), F32)).astype(BF16)
    perms = _row_order_perms(bsz)
    h = x.reshape(rows, D_MODEL)
    for l in range(depth):
        mod3 = _mod(c, ada_w[l], ada_b[l]).reshape(bsz, 6, D_MODEL)
        mod_rows = jnp.broadcast_to(mod3.transpose(1, 0, 2)[:, :, None, :], (6, bsz, SUBLANES, D_MODEL))
        kv0 = RWKV_COLS + ATTN_DIM
        kv_cols = kv0 + (jnp.arange(4 * KV_DIM) // (2 * HEAD_DIM)) * HEAD_DIM + jnp.arange(4 * KV_DIM) % HEAD_DIM
        w_proj = jnp.concatenate([w_in[l][:, :kv0], w_in[l][:, kv_cols], w_in[l][:, kv0 + 2 * KV_DIM:]], axis=1)
        f_rwkv, q, kv, gate_a, gate_b = _in_proj(h, mod3, norm1_g[l], w_proj.astype(BF16), tiles_per_seq)

        lora_w = jnp.zeros((LORA_COLS, 2 * RWKV_DIM), F32)
        lora_w = lora_w.at[:DECAY_LORA, :RWKV_DIM].set(rwkv_w_up[l]).at[DECAY_LORA:, RWKV_DIM:].set(rwkv_a_up[l])
        vecs = jnp.stack([rwkv_w0[l], rwkv_a0[l], rwkv_k_k[l], rwkv_k_a[l], rwkv_r_k[l].reshape(RWKV_DIM)])
        scan_in, pend, g, bonus = _prep(f_rwkv.reshape(bsz, seq, RWKV_COLS), rwkv_mu[l].reshape(1, RWKV_COLS),
                                        lora_w, rwkv_g_up[l], vecs, ones_bd, perms)
        yn = _scan(scan_in.reshape(SCAN_INPUTS, seq, bsz, RWKV_DIM), pend)

        yb = _attn(q, kv, attn_sinks[l], seq // ATTN_BLOCK)

        gn = jnp.stack([rwkv_gn_w[l], rwkv_gn_b[l]])
        by_batch = lambda t: t.reshape(bsz, seq, t.shape[-1])
        h3, u2 = _merge(by_batch(h), mod_rows, yn, bonus, g, by_batch(yb), by_batch(gate_a), by_batch(gate_b),
                        gn, norm2_g[l], w_branch_a[l].astype(BF16), w_branch_b[l].astype(BF16),
                        w_out[l].astype(BF16), perms)
        w_up = ffn_w_up[l].astype(BF16)
        h = _ffn(h3.reshape(rows, D_MODEL), u2.reshape(rows, D_MODEL), mod3, w_up[:, :D_FF], w_up[:, D_FF:],
                 ffn_conv_w[l], ffn_conv_b[l], ffn_w_down[l].astype(BF16), final_g, tiles_per_seq)
    return h.reshape(bsz, seq, D_MODEL)
```

```python
import functools

import jax
import jax.numpy as jnp
from jax.experimental import pallas as pl
from jax.experimental.pallas import tpu as pltpu

F32 = jnp.float32
BF16 = jnp.bfloat16

D_MODEL = 1024
RWKV_HEADS = 8
HEAD_DIM = 64
RWKV_DIM = RWKV_HEADS * HEAD_DIM
DECAY_LORA = 64
AAA_LORA = 64
GATE_LORA = 128
LORA_COLS = DECAY_LORA + AAA_LORA
RWKV_COLS = 3 * RWKV_DIM + LORA_COLS + GATE_LORA
ATTN_HEADS = 8
ATTN_KV_HEADS = 2
ATTN_GROUP = ATTN_HEADS // ATTN_KV_HEADS
ATTN_DIM = ATTN_HEADS * HEAD_DIM
KV_DIM = ATTN_KV_HEADS * HEAD_DIM
ATTN_BLOCK = 128
ATTN_BLOCKS_PER_STEP = 4
D_FF = 2816
GN_EPS = 64e-5
RMS_EPS = 1e-6
DECAY_SCALE = 0.6065306597126334

VMEM_LIMIT_BYTES = 56 * 1024 * 1024
ROW_TILE = 512
FFN_CHUNK = 256
FFN_LOOKAHEAD = 2
SCAN_STEPS = 32
SCAN_INPUTS = 5
UPDATE_GROUPS = 2
SUBLANES = 8
NEG_BIG = -1e30


def _params(*semantics):
    return pltpu.CompilerParams(dimension_semantics=semantics, vmem_limit_bytes=VMEM_LIMIT_BYTES)


def _resident(shape):
    zeros = (0,) * len(shape)
    return pl.BlockSpec(shape, lambda *_: zeros, pipeline_mode=pl.Buffered(1))


def _bdot(a, b):
    return jnp.dot(a.astype(BF16), b.astype(BF16), preferred_element_type=F32)


def _split(x):
    hi = x.astype(BF16)
    lo = (x - hi.astype(F32)).astype(BF16)
    return hi, lo


def _dot3(a, b):
    a_hi, a_lo = _split(a)
    b_hi, b_lo = _split(b)
    dot = functools.partial(jnp.dot, preferred_element_type=F32)
    return dot(a_hi, b_hi) + dot(a_lo, b_hi) + dot(a_hi, b_lo)


def _head_sum(x, ones_blockdiag):
    hi, lo = _split(x)
    dot = functools.partial(jnp.dot, preferred_element_type=F32)
    return dot(hi, ones_blockdiag) + dot(lo, ones_blockdiag)


def _rms_mod(h, gain, scale, shift):
    y = h * jax.lax.rsqrt(jnp.mean(h * h, axis=-1, keepdims=True) + RMS_EPS)
    return y * gain * (1.0 + scale) + shift


def _permute_rows(p, x, pieces=3):
    dot = functools.partial(jnp.dot, preferred_element_type=F32)
    out = None
    for _ in range(pieces):
        piece = x.astype(BF16)
        x = x - piece.astype(F32)
        out = dot(p, piece) if out is None else out + dot(p, piece)
    return out


def _row_order_perms(bsz):
    rows = bsz * SUBLANES
    src = jnp.arange(rows)[None, :]
    to_time = jnp.arange(rows)[:, None] == (src % SUBLANES) * bsz + src // SUBLANES
    return jnp.stack([to_time, to_time.T]).astype(BF16)


def _per_batch_rows(t, bsz):
    groups = ROW_TILE // bsz // SUBLANES
    return jnp.broadcast_to(t[:, None], (bsz, groups) + t.shape[1:]).reshape(ROW_TILE, t.shape[-1])


def _mod_kernel(c_ref, w_ref, b_ref, o_ref):
    c = c_ref[...]
    o_ref[...] = _bdot(c * jax.nn.sigmoid(c), w_ref[...]) + b_ref[...]


def _mod(c, ada_w, ada_b):
    bsz = c.shape[0]
    cols = ada_w.shape[1]
    tile = cols // 4
    return pl.pallas_call(
        _mod_kernel,
        grid=(cols // tile,),
        in_specs=[pl.BlockSpec((bsz, D_MODEL), lambda j: (0, 0)),
                  pl.BlockSpec((D_MODEL, tile), lambda j: (0, j)),
                  pl.BlockSpec((1, tile), lambda j: (0, j))],
        out_specs=pl.BlockSpec((bsz, tile), lambda j: (0, j)),
        out_shape=jax.ShapeDtypeStruct((bsz, cols), F32),
        compiler_params=_params("arbitrary"),
        name="mod",
    )(c, ada_w, ada_b.reshape(1, cols))


def _in_proj_kernel(x_ref, mod_ref, g_ref, w_ref, f_ref, q_ref, kv_ref, ga_ref, gb_ref):
    u = _rms_mod(x_ref[...], g_ref[...], mod_ref[0, 1:2, :], mod_ref[0, 0:1, :]).astype(BF16)
    col = 0
    for out in (f_ref, q_ref, kv_ref, ga_ref, gb_ref):
        width = out.shape[1]
        out[...] = jnp.dot(u, w_ref[:, col:col + width], preferred_element_type=F32).astype(out.dtype)
        col += width


def _in_proj(x2, mod3, norm_g, w_in, tiles_per_seq):
    rows = x2.shape[0]
    widths = (RWKV_COLS, ATTN_DIM, 4 * KV_DIM, D_MODEL, D_MODEL)
    dtypes = (F32, BF16, BF16, BF16, BF16)
    return pl.pallas_call(
        _in_proj_kernel,
        grid=(rows // ROW_TILE,),
        in_specs=[pl.BlockSpec((ROW_TILE, D_MODEL), lambda i: (i, 0)),
                  pl.BlockSpec((1, 6, D_MODEL), lambda i: (i // tiles_per_seq, 0, 0)),
                  _resident((1, D_MODEL)),
                  _resident(w_in.shape)],
        out_specs=[pl.BlockSpec((ROW_TILE, w), lambda i: (i, 0)) for w in widths],
        out_shape=[jax.ShapeDtypeStruct((rows, w), d) for w, d in zip(widths, dtypes)],
        compiler_params=_params("arbitrary"),
        name="in_proj",
    )(x2, mod3, norm_g.reshape(1, D_MODEL), w_in)


def _prep_kernel(f_ref, prev_ref, mu_ref, lora_ref, gup_ref, vec_ref, ones_ref, perm_ref,
                 x_ref, pend_ref, g_ref, bonus_ref, group_ref):
    bsz, steps, cols = f_ref.shape
    feat = f_ref[...].reshape(ROW_TILE, cols)
    before = pltpu.roll(prev_ref[...].reshape(bsz * SUBLANES, cols), bsz * SUBLANES - (SUBLANES - 1), 0)
    before = jnp.where(pl.program_id(0) == 0, 0.0, before)
    before = _per_batch_rows(before.reshape(bsz, SUBLANES, cols), bsz)
    row = jax.lax.broadcasted_iota(jnp.int32, feat.shape, 0)
    shifted = jnp.where(row % steps == 0, before, pltpu.roll(feat, 1, 0))
    f = feat + (shifted - feat) * mu_ref[...]

    r = f[:, 0:RWKV_DIM]
    k = f[:, RWKV_DIM:2 * RWKV_DIM]
    v = f[:, 2 * RWKV_DIM:3 * RWKV_DIM]
    lo = f[:, 3 * RWKV_DIM:3 * RWKV_DIM + LORA_COLS]
    g_lo = f[:, 3 * RWKV_DIM + LORA_COLS:]
    w0, a0, k_k, k_a, r_k = (vec_ref[i:i + 1, :] for i in range(5))

    lane = jax.lax.broadcasted_iota(jnp.int32, lo.shape, 1)
    lora = _dot3(jnp.where(lane < DECAY_LORA, jnp.tanh(lo), lo), lora_ref[...])
    log_decay = -DECAY_SCALE * jax.nn.sigmoid(w0 + lora[:, :RWKV_DIM])
    step_in_tile = jax.lax.broadcasted_iota(jnp.int32, (ROW_TILE, RWKV_DIM), 0) % steps
    cum = log_decay
    shift = 1
    while shift < steps:
        cum = cum + jnp.where(step_in_tile >= shift, pltpu.roll(cum, shift, 0), 0.0)
        shift *= 2
    decayed = jnp.exp(cum)
    grown = jnp.exp(-cum)
    decayed_before = jnp.exp(cum - log_decay)
    a = jax.nn.sigmoid(a0 + lora[:, RWKV_DIM:])
    g = _bdot(jax.nn.sigmoid(g_lo), gup_ref[...])

    ones = ones_ref[...]
    kk = k * k_k
    kk = kk / jnp.maximum(jnp.sqrt(_head_sum(kk * kk, ones)), 1e-12)
    k = k * (1.0 + (a - 1.0) * k_a)

    to_time = perm_ref[0]
    group_rows = bsz * SUBLANES
    groups = steps // SUBLANES

    def time_major(grp, pieces):
        return _permute_rows(to_time, group_ref[:, grp].reshape(group_rows, RWKV_DIM), pieces)

    scan_inputs = (r * decayed, k * grown, v, kk * decayed_before, kk * a * grown)
    for i, t in enumerate(scan_inputs):
        group_ref[...] = t.reshape(bsz, groups, SUBLANES, RWKV_DIM)
        for grp in range(groups):
            x_ref[i, grp * group_rows:(grp + 1) * group_rows, :] = time_major(grp, 2)
    group_ref[...] = decayed.reshape(bsz, groups, SUBLANES, RWKV_DIM)
    pend_ref[...] = time_major(groups - 1, 3)[(SUBLANES - 1) * bsz:]
    g_ref[...] = g.astype(BF16).reshape(bsz, steps, RWKV_DIM)
    bonus_ref[...] = (_bdot(r * k * r_k, ones) * v).astype(BF16).reshape(bsz, steps, RWKV_DIM)


def _prep(f_rwkv, mu, lora_w, g_up, vecs, ones_bd, perms):
    bsz, seq, cols = f_rwkv.shape
    steps = ROW_TILE // bsz
    batch_major = pl.BlockSpec((bsz, steps, RWKV_DIM), lambda i: (0, i, 0))
    return pl.pallas_call(
        _prep_kernel,
        grid=(seq // steps,),
        in_specs=[pl.BlockSpec((bsz, steps, cols), lambda i: (0, i, 0)),
                  pl.BlockSpec((bsz, SUBLANES, cols),
                               lambda i: (0, jnp.maximum(i * (steps // SUBLANES) - 1, 0), 0)),
                  _resident(mu.shape), _resident(lora_w.shape), _resident(g_up.shape),
                  _resident(vecs.shape), _resident(ones_bd.shape), _resident(perms.shape)],
        out_specs=[pl.BlockSpec((SCAN_INPUTS, ROW_TILE, RWKV_DIM), lambda i: (0, i, 0)),
                   pl.BlockSpec((bsz, RWKV_DIM), lambda i: (i, 0)),
                   batch_major, batch_major],
        out_shape=[jax.ShapeDtypeStruct((SCAN_INPUTS, seq * bsz, RWKV_DIM), F32),
                   jax.ShapeDtypeStruct((seq // steps * bsz, RWKV_DIM), F32)]
        + [jax.ShapeDtypeStruct((bsz, seq, RWKV_DIM), BF16)] * 2,
        scratch_shapes=[pltpu.VMEM((bsz, steps // SUBLANES, SUBLANES, RWKV_DIM), F32)],
        compiler_params=_params("arbitrary"),
        name="rwkv_prep",
    )(f_rwkv, f_rwkv, mu, lora_w, g_up, vecs, ones_bd, perms)


def _scan_kernel(x_ref, pend_ref, y_ref, state_ref, rows_even_ref, rows_odd_ref, yraw_ref):
    pairs = state_ref.shape[-1]
    bsz = x_ref.shape[2]
    two_heads = 2 * HEAD_DIM
    head_pairs = [slice(hp * two_heads, (hp + 1) * two_heads) for hp in range(RWKV_HEADS // 2)]
    R, K, V, KK, B = range(SCAN_INPUTS)
    per_group = HEAD_DIM // UPDATE_GROUPS
    in_update = (SCAN_INPUTS - 1) // UPDATE_GROUPS

    def regroup(slab_of):
        xt = jnp.concatenate([slab_of(lanes) for lanes in head_pairs], axis=0).T
        xt = jnp.concatenate([xt[:HEAD_DIM], xt[HEAD_DIM:]], axis=1)
        return xt.reshape(SUBLANES, SUBLANES, pairs)

    def transpose_in(i, t, rows_ref):
        rows_ref[i] = regroup(lambda lanes: x_ref[i, t, :, lanes])

    def write_out(t):
        y = yraw_ref[...]
        cen = y - jnp.mean(y, axis=0, keepdims=True)
        var = jnp.mean(cen * cen, axis=0, keepdims=True)
        yn = cen * jax.lax.rsqrt(var + GN_EPS)
        slab = jnp.concatenate([yn[:, :pairs // 2], yn[:, pairs // 2:]], axis=0).T
        for hp, lanes in enumerate(head_pairs):
            y_ref[t, :, lanes] = slab[hp * bsz:(hp + 1) * bsz]

    @pl.when(pl.program_id(0) == 0)
    def _():
        state_ref[...] = jnp.zeros_like(state_ref)
        yraw_ref[...] = jnp.zeros_like(yraw_ref)

    pend = regroup(lambda lanes: pend_ref[:, lanes]).reshape(HEAD_DIM, pairs)
    for j in range(HEAD_DIM):
        state_ref[j] = state_ref[j] * pend[j:j + 1, :]

    for i in range(SCAN_INPUTS):
        transpose_in(i, 0, rows_even_ref)

    def step(t, rows_ref, next_rows_ref):
        t_next = jnp.minimum(t + 1, SCAN_STEPS - 1)
        write_out(jnp.maximum(t - 1, 0))
        for i in range(in_update * UPDATE_GROUPS, SCAN_INPUTS):
            transpose_in(i, t_next, next_rows_ref)

        acc = jnp.zeros((HEAD_DIM, pairs), F32)
        for j in range(HEAD_DIM):
            acc = acc + state_ref[j] * rows_ref[KK, j // SUBLANES, j % SUBLANES:j % SUBLANES + 1, :]
        sa = -acc
        v = rows_ref[V].reshape(HEAD_DIM, pairs)

        def update(grp, y):
            for u in range(in_update):
                transpose_in(grp * in_update + u, t_next, next_rows_ref)
            for jj in range(per_group):
                j = grp * per_group + jj
                jh = grp * (per_group // SUBLANES) + jj // SUBLANES
                kl = slice(jj % SUBLANES, jj % SUBLANES + 1)
                s_new = state_ref[j] + sa * rows_ref[B, jh, kl, :] + v * rows_ref[K, jh, kl, :]
                state_ref[j] = s_new
                y = y + s_new * rows_ref[R, jh, kl, :]
            return y

        yraw_ref[...] = jax.lax.fori_loop(0, UPDATE_GROUPS, update, jnp.zeros((HEAD_DIM, pairs), F32))

    def two_steps(i, carry):
        step(2 * i, rows_even_ref, rows_odd_ref)
        step(2 * i + 1, rows_odd_ref, rows_even_ref)
        return carry

    jax.lax.fori_loop(0, SCAN_STEPS // 2, two_steps, 0)
    write_out(SCAN_STEPS - 1)


def _scan(x, pend):
    _, seq, bsz, _ = x.shape
    pairs = bsz * RWKV_HEADS
    return pl.pallas_call(
        _scan_kernel,
        grid=(seq // SCAN_STEPS,),
        in_specs=[pl.BlockSpec((SCAN_INPUTS, SCAN_STEPS, bsz, RWKV_DIM), lambda i: (0, i, 0, 0)),
                  pl.BlockSpec((bsz, RWKV_DIM), lambda i: (jnp.maximum(i - 1, 0), 0))],
        out_specs=pl.BlockSpec((SCAN_STEPS, bsz, RWKV_DIM), lambda i: (i, 0, 0)),
        out_shape=jax.ShapeDtypeStruct((seq, bsz, RWKV_DIM), F32),
        scratch_shapes=[pltpu.VMEM((HEAD_DIM, HEAD_DIM, pairs), F32),
                        pltpu.VMEM((SCAN_INPUTS, SUBLANES, SUBLANES, pairs), F32),
                        pltpu.VMEM((SCAN_INPUTS, SUBLANES, SUBLANES, pairs), F32),
                        pltpu.VMEM((HEAD_DIM, pairs), F32)],
        compiler_params=_params("arbitrary"),
        name="rwkv_scan",
    )(x, pend)


def _attn_kernel(q_ref, kv_ref, kvp_ref, bias_first_ref, bias_ref, sink_ref, o_ref):
    kv_all = jnp.concatenate([kvp_ref[...], kv_ref[...]], axis=0)
    for blk in range(ATTN_BLOCKS_PER_STEP):
        rows = slice(blk * ATTN_BLOCK, (blk + 1) * ATTN_BLOCK)
        _attend_block(q_ref.at[rows], kv_all[blk * ATTN_BLOCK:(blk + 2) * ATTN_BLOCK],
                      bias_first_ref if blk == 0 else bias_ref, sink_ref, o_ref.at[rows])


def _attend_block(q_ref, kv, bias_ref, sink_ref, o_ref):
    low = jax.lax.broadcasted_iota(jnp.int32, (2 * ATTN_BLOCK, 2 * HEAD_DIM), 1) < HEAD_DIM
    low_out = jax.lax.broadcasted_iota(jnp.int32, (ATTN_BLOCK, 2 * HEAD_DIM), 1) < HEAD_DIM
    nothing = jnp.zeros((2 * ATTN_BLOCK, 2 * HEAD_DIM), BF16)

    def block_diag(t):
        return jnp.concatenate([jnp.where(low, t, nothing), jnp.where(low, nothing, t)], axis=0)

    pairs = range(ATTN_HEADS // 2)
    pairs_per_kv = ATTN_GROUP // 2
    lanes = [slice(n * 2 * HEAD_DIM, (n + 1) * 2 * HEAD_DIM) for n in range(2 * ATTN_KV_HEADS)]
    k_pairs = [block_diag(kv[:, lanes[kvh]]) for kvh in range(ATTN_KV_HEADS)]
    v_pairs = [block_diag(kv[:, lanes[ATTN_KV_HEADS + kvh]]) for kvh in range(ATTN_KV_HEADS)]
    scores = []
    for pair in pairs:
        q2 = q_ref[:, lanes[pair]] * (HEAD_DIM ** -0.5)
        s = jax.lax.dot_general(q2, k_pairs[pair // pairs_per_kv], (((1,), (1,)), ((), ())),
                                preferred_element_type=F32)
        scores.append(s + bias_ref[0, pair])
    probs, scales = [], []
    for pair in pairs:
        halves, inv = [], []
        for half in range(2):
            sh = scores[pair][:, half * 2 * ATTN_BLOCK:(half + 1) * 2 * ATTN_BLOCK]
            sink = sink_ref[2 * pair + half]
            m = jnp.maximum(jnp.max(sh, axis=-1, keepdims=True), sink)
            p = jnp.exp(sh - m)
            inv.append(1.0 / (jnp.sum(p, axis=-1, keepdims=True) + jnp.exp(sink - m)))
            halves.append(p.astype(BF16))
        probs.append(jnp.concatenate(halves, axis=1))
        scales.append(jnp.where(low_out, inv[0], inv[1]))
    for pair in pairs:
        o = jnp.dot(probs[pair], v_pairs[pair // pairs_per_kv], preferred_element_type=F32)
        o_ref[:, lanes[pair]] = (o * scales[pair]).astype(o_ref.dtype)


def _attn_bias():
    qi = jnp.arange(ATTN_BLOCK)[:, None]
    sj = jnp.arange(2 * ATTN_BLOCK)[None, :]
    dist = qi + ATTN_BLOCK - sj
    in_window = (dist >= 0) & (dist < ATTN_BLOCK)
    slopes = 2.0 ** (-8.0 * jnp.arange(1, ATTN_HEADS + 1, dtype=F32) / ATTN_HEADS)
    alibi = -slopes[:, None, None] * dist.astype(F32)
    valid = jnp.stack([in_window & (sj >= ATTN_BLOCK), in_window])[:, None]
    bias = jnp.where(valid, alibi[None], NEG_BIG)
    bias = bias.reshape(2, ATTN_HEADS // 2, 2, ATTN_BLOCK, 2 * ATTN_BLOCK).transpose(0, 1, 3, 2, 4)
    return bias.reshape(2, ATTN_HEADS // 2, ATTN_BLOCK, 4 * ATTN_BLOCK)


def _attn(q, kv, sinks, blocks_per_seq):
    rows = q.shape[0]
    bias = _attn_bias()
    per_step = ATTN_BLOCKS_PER_STEP
    steps_per_seq = blocks_per_seq // per_step
    step_rows = per_step * ATTN_BLOCK
    return pl.pallas_call(
        _attn_kernel,
        grid=(rows // step_rows // steps_per_seq, steps_per_seq),
        in_specs=[pl.BlockSpec((step_rows, ATTN_DIM), lambda b, i: (b * steps_per_seq + i, 0)),
                  pl.BlockSpec((step_rows, 4 * KV_DIM), lambda b, i: (b * steps_per_seq + i, 0)),
                  pl.BlockSpec((ATTN_BLOCK, 4 * KV_DIM),
                               lambda b, i: (b * blocks_per_seq + jnp.maximum(i * per_step - 1, 0), 0)),
                  pl.BlockSpec((1,) + bias.shape[1:], lambda b, i: (jnp.minimum(i, 1), 0, 0, 0)),
                  pl.BlockSpec((1,) + bias.shape[1:], lambda b, i: (1, 0, 0, 0)),
                  pl.BlockSpec(memory_space=pltpu.SMEM)],
        out_specs=pl.BlockSpec((step_rows, ATTN_DIM), lambda b, i: (b * steps_per_seq + i, 0)),
        out_shape=jax.ShapeDtypeStruct((rows, ATTN_DIM), BF16),
        compiler_params=_params("arbitrary", "arbitrary"),
        name="swa_attn",
    )(q, kv, kv, bias, bias, sinks)


def _merge_kernel(x_ref, mod_ref, yn_ref, bonus_ref, g_ref, yb_ref, ga_ref, gb_ref, gn_ref, n2_ref,
                  wa_ref, wb_ref, wo_ref, perm_ref, h_ref, u_ref, group_ref):
    bsz, steps, _ = x_ref.shape
    group_rows = bsz * SUBLANES
    to_batch = perm_ref[1]
    for grp in range(steps // SUBLANES):
        rows = yn_ref[grp * SUBLANES:(grp + 1) * SUBLANES].reshape(group_rows, RWKV_DIM)
        group_ref[:, grp] = _permute_rows(to_batch, rows).reshape(bsz, SUBLANES, RWKV_DIM)
    yn = group_ref[...].reshape(ROW_TILE, RWKV_DIM)

    def tile(ref):
        return ref[...].reshape(ROW_TILE, ref.shape[-1])

    ya = ((yn * gn_ref[0:1, :] + gn_ref[1:2, :]) + tile(bonus_ref).astype(F32)) * tile(g_ref).astype(F32)
    pa = _bdot(ya, wa_ref[...])
    pb = jnp.dot(tile(yb_ref), wb_ref[...], preferred_element_type=F32)
    merged = (jax.nn.sigmoid(tile(ga_ref).astype(F32)) * pa
              + jax.nn.sigmoid(tile(gb_ref).astype(F32)) * pb)
    h = tile(x_ref) + _per_batch_rows(mod_ref[2], bsz) * _bdot(merged, wo_ref[...])
    h_ref[...] = h.reshape(h_ref.shape)
    u = _rms_mod(h, n2_ref[...], _per_batch_rows(mod_ref[4], bsz), _per_batch_rows(mod_ref[3], bsz))
    u_ref[...] = u.astype(BF16).reshape(u_ref.shape)


def _merge(x3, mod_rows, yn, bonus, g, yb, ga, gb, gn, norm2_g, wa, wb, wo, perms):
    bsz, seq, _ = x3.shape
    steps = ROW_TILE // bsz

    def tile(width):
        return pl.BlockSpec((bsz, steps, width), lambda i: (0, i, 0))

    return pl.pallas_call(
        _merge_kernel,
        grid=(seq // steps,),
        in_specs=[tile(D_MODEL), _resident(mod_rows.shape),
                  pl.BlockSpec((steps, bsz, RWKV_DIM), lambda i: (i, 0, 0)),
                  tile(RWKV_DIM), tile(RWKV_DIM), tile(ATTN_DIM), tile(D_MODEL), tile(D_MODEL),
                  _resident(gn.shape), _resident((1, D_MODEL)),
                  _resident(wa.shape), _resident(wb.shape), _resident(wo.shape), _resident(perms.shape)],
        out_specs=[tile(D_MODEL), tile(D_MODEL)],
        out_shape=[jax.ShapeDtypeStruct((bsz, seq, D_MODEL), F32),
                   jax.ShapeDtypeStruct((bsz, seq, D_MODEL), BF16)],
        scratch_shapes=[pltpu.VMEM((bsz, steps // SUBLANES, SUBLANES, RWKV_DIM), F32)],
        compiler_params=_params("arbitrary"),
        name="merge",
    )(x3, mod_rows, yn, bonus, g, yb, ga, gb, gn, norm2_g.reshape(1, D_MODEL), wa, wb, wo, perms)


def _ffn_kernel(tiles_per_seq, h_ref, u_ref, mod_ref, wg_ref, wv_ref, cw_ref, cb_ref, wd_ref, fg_ref,
                o_ref, carry_ref, act_ref):
    @pl.when(pl.program_id(0) % tiles_per_seq == 0)
    def _():
        carry_ref[...] = jnp.zeros_like(carry_ref)

    u = u_ref[...]
    row = jax.lax.broadcasted_iota(jnp.int32, (ROW_TILE, FFN_CHUNK), 0)
    chunks = [slice(c * FFN_CHUNK, (c + 1) * FFN_CHUNK) for c in range(D_FF // FFN_CHUNK)]

    def up(cols):
        return (jnp.dot(u, wg_ref[:, cols], preferred_element_type=F32),
                jnp.dot(u, wv_ref[:, cols], preferred_element_type=F32))

    ahead = [up(cols) for cols in chunks[:FFN_LOOKAHEAD]]
    for c, cols in enumerate(chunks):
        gate, val = ahead.pop(0)
        if c + FFN_LOOKAHEAD < len(chunks):
            ahead.append(up(chunks[c + FFN_LOOKAHEAD]))
        back1 = carry_ref[SUBLANES - 1:SUBLANES, cols]
        back2 = carry_ref[SUBLANES - 2:SUBLANES - 1, cols]
        carry_ref[:, cols] = gate[ROW_TILE - SUBLANES:, :]
        shift1 = jnp.where(row == 0, back1, pltpu.roll(gate, 1, 0))
        shift2 = jnp.where(row == 0, back2, jnp.where(row == 1, back1, pltpu.roll(gate, 2, 0)))
        conv = (cw_ref[0:1, cols] * shift2 + cw_ref[1:2, cols] * shift1 + cw_ref[2:3, cols] * gate
                + cb_ref[:, cols])
        act_ref[:, cols] = (conv * jax.nn.sigmoid(conv) * val).astype(BF16)
    acc = jnp.dot(act_ref[...], wd_ref[...], preferred_element_type=F32)
    h = h_ref[...] + mod_ref[0, 5:6, :] * acc
    o_ref[...] = h * jax.lax.rsqrt(jnp.mean(h * h, axis=-1, keepdims=True) + RMS_EPS) * fg_ref[...]


def _ffn(h1, u2, mod3, w_gate, w_val, conv_w, conv_b, w_down, final_g, tiles_per_seq):
    rows = h1.shape[0]
    return pl.pallas_call(
        functools.partial(_ffn_kernel, tiles_per_seq),
        grid=(rows // ROW_TILE,),
        in_specs=[pl.BlockSpec((ROW_TILE, D_MODEL), lambda i: (i, 0)),
                  pl.BlockSpec((ROW_TILE, D_MODEL), lambda i: (i, 0)),
                  pl.BlockSpec((1, 6, D_MODEL), lambda i: (i // tiles_per_seq, 0, 0)),
                  _resident(w_gate.shape), _resident(w_val.shape), _resident(conv_w.shape),
                  _resident((1, D_FF)), _resident(w_down.shape), _resident((1, D_MODEL))],
        out_specs=pl.BlockSpec((ROW_TILE, D_MODEL), lambda i: (i, 0)),
        out_shape=jax.ShapeDtypeStruct((rows, D_MODEL), F32),
        scratch_shapes=[pltpu.VMEM((SUBLANES, D_FF), F32), pltpu.VMEM((ROW_TILE, D_FF), BF16)],
        compiler_params=_params("arbitrary"),
        name="ffn",
    )(h1, u2, mod3, w_gate, w_val, conv_w, conv_b.reshape(1, D_FF), w_down, final_g.reshape(1, D_MODEL))


def kernel(x, c, ada_w, ada_b, norm1_g, w_in, rwkv_mu, rwkv_w0, rwkv_w_up, rwkv_a0, rwkv_a_up, rwkv_g_up, rwkv_k_k, rwkv_k_a, rwkv_r_k, rwkv_gn_w, rwkv_gn_b, attn_sinks, w_branch_a, w_branch_b, w_out, norm2_g, ffn_w_up, ffn_conv_w, ffn_conv_b, ffn_w_down, final_g):
    bsz, seq, _ = x.shape
    depth = ada_w.shape[0]
    rows = bsz * seq
    assert depth == 1, "the ffn kernel fuses the final norm, which follows the last layer only"
    assert bsz * RWKV_HEADS == 128 and seq % ROW_TILE == 0 and seq % SCAN_STEPS == 0
    assert ROW_TILE == bsz * SCAN_STEPS, "prep restarts the decay product per tile, the scan per block"
    tiles_per_seq = seq // ROW_TILE
    ones_bd = jnp.kron(jnp.eye(RWKV_HEADS, dtype=F32), jnp.ones((HEAD_DIM, HEAD_DIM), F32)).astype(BF16)
    perms = _row_order_perms(bsz)
    h = x.reshape(rows, D_MODEL)
    for l in range(depth):
        mod3 = _mod(c, ada_w[l], ada_b[l]).reshape(bsz, 6, D_MODEL)
        mod_rows = jnp.broadcast_to(mod3.transpose(1, 0, 2)[:, :, None, :], (6, bsz, SUBLANES, D_MODEL))
        kv0 = RWKV_COLS + ATTN_DIM
        kv_cols = kv0 + (jnp.arange(4 * KV_DIM) // (2 * HEAD_DIM)) * HEAD_DIM + jnp.arange(4 * KV_DIM) % HEAD_DIM
        w_proj = jnp.concatenate([w_in[l][:, :kv0], w_in[l][:, kv_cols], w_in[l][:, kv0 + 2 * KV_DIM:]], axis=1)
        f_rwkv, q, kv, gate_a, gate_b = _in_proj(h, mod3, norm1_g[l], w_proj.astype(BF16), tiles_per_seq)

        lora_w = jnp.zeros((LORA_COLS, 2 * RWKV_DIM), F32)
        lora_w = lora_w.at[:DECAY_LORA, :RWKV_DIM].set(rwkv_w_up[l]).at[DECAY_LORA:, RWKV_DIM:].set(rwkv_a_up[l])
        vecs = jnp.stack([rwkv_w0[l], rwkv_a0[l], rwkv_k_k[l], rwkv_k_a[l], rwkv_r_k[l].reshape(RWKV_DIM)])
        scan_in, pend, g, bonus = _prep(f_rwkv.reshape(bsz, seq, RWKV_COLS), rwkv_mu[l].reshape(1, RWKV_COLS),
                                        lora_w, rwkv_g_up[l], vecs, ones_bd, perms)
        yn = _scan(scan_in.reshape(SCAN_INPUTS, seq, bsz, RWKV_DIM), pend)

        yb = _attn(q, kv, attn_sinks[l], seq // ATTN_BLOCK)

        gn = jnp.stack([rwkv_gn_w[l], rwkv_gn_b[l]])
        by_batch = lambda t: t.reshape(bsz, seq, t.shape[-1])
        h3, u2 = _merge(by_batch(h), mod_rows, yn, bonus, g, by_batch(yb), by_batch(gate_a), by_batch(gate_b),
                        gn, norm2_g[l], w_branch_a[l].astype(BF16), w_branch_b[l].astype(BF16),
                        w_out[l].astype(BF16), perms)
        w_up = ffn_w_up[l].astype(BF16)
        h = _ffn(h3.reshape(rows, D_MODEL), u2.reshape(rows, D_MODEL), mod3, w_up[:, :D_FF], w_up[:, D_FF:],
                 ffn_conv_w[l], ffn_conv_b[l], ffn_w_down[l].astype(BF16), final_g, tiles_per_seq)
    return h.reshape(bsz, seq, D_MODEL)
```

```python
import functools

import jax
import jax.numpy as jnp
from jax.experimental import pallas as pl
from jax.experimental.pallas import tpu as pltpu

F32 = jnp.float32
BF16 = jnp.bfloat16

D_MODEL = 1024
RWKV_HEADS = 8
HEAD_DIM = 64
RWKV_DIM = RWKV_HEADS * HEAD_DIM
DECAY_LORA = 64
AAA_LORA = 64
GATE_LORA = 128
LORA_COLS = DECAY_LORA + AAA_LORA
RWKV_COLS = 3 * RWKV_DIM + LORA_COLS + GATE_LORA
ATTN_HEADS = 8
ATTN_KV_HEADS = 2
ATTN_GROUP = ATTN_HEADS // ATTN_KV_HEADS
ATTN_DIM = ATTN_HEADS * HEAD_DIM
KV_DIM = ATTN_KV_HEADS * HEAD_DIM
ATTN_BLOCK = 128
ATTN_BLOCKS_PER_STEP = 4
D_FF = 2816
GN_EPS = 64e-5
RMS_EPS = 1e-6
DECAY_SCALE = 0.6065306597126334

VMEM_LIMIT_BYTES = 56 * 1024 * 1024
ROW_TILE = 512
FFN_CHUNK = 256
FFN_LOOKAHEAD = 2
SCAN_STEPS = 32
SCAN_INPUTS = 5
UPDATE_GROUPS = 2
SUBLANES = 8
NEG_BIG = -1e30


def _params(*semantics):
    return pltpu.CompilerParams(dimension_semantics=semantics, vmem_limit_bytes=VMEM_LIMIT_BYTES)


def _resident(shape):
    zeros = (0,) * len(shape)
    return pl.BlockSpec(shape, lambda *_: zeros, pipeline_mode=pl.Buffered(1))


def _bdot(a, b):
    return jnp.dot(a.astype(BF16), b.astype(BF16), preferred_element_type=F32)


def _split(x):
    hi = x.astype(BF16)
    lo = (x - hi.astype(F32)).astype(BF16)
    return hi, lo


def _head_sum(x, ones_blockdiag):
    hi, lo = _split(x)
    dot = functools.partial(jnp.dot, preferred_element_type=F32)
    return dot(hi, ones_blockdiag) + dot(lo, ones_blockdiag)


def _rms_mod(h, gain, scale, shift):
    y = h * jax.lax.rsqrt(jnp.mean(h * h, axis=-1, keepdims=True) + RMS_EPS)
    return y * (gain * (1.0 + scale)) + shift


def _permute_rows(p, x, pieces=3):
    dot = functools.partial(jnp.dot, preferred_element_type=F32)
    out = None
    for _ in range(pieces):
        piece = x.astype(BF16)
        x = x - piece.astype(F32)
        out = dot(p, piece) if out is None else out + dot(p, piece)
    return out


def _row_order_perms(bsz):
    rows = bsz * SUBLANES
    src = jnp.arange(rows)[None, :]
    to_time = jnp.arange(rows)[:, None] == (src % SUBLANES) * bsz + src // SUBLANES
    return jnp.stack([to_time, to_time.T]).astype(BF16)


def _per_batch_rows(t, bsz):
    groups = ROW_TILE // bsz // SUBLANES
    return jnp.broadcast_to(t[:, None], (bsz, groups) + t.shape[1:]).reshape(ROW_TILE, t.shape[-1])


def _mod_kernel(c_ref, w_ref, b_ref, o_ref):
    c = c_ref[...]
    o_ref[...] = _bdot(c * jax.nn.sigmoid(c), w_ref[...]) + b_ref[...]


def _mod(c, ada_w, ada_b):
    bsz = c.shape[0]
    cols = ada_w.shape[1]
    tile = cols // 4
    return pl.pallas_call(
        _mod_kernel,
        grid=(cols // tile,),
        in_specs=[pl.BlockSpec((bsz, D_MODEL), lambda j: (0, 0)),
                  pl.BlockSpec((D_MODEL, tile), lambda j: (0, j)),
                  pl.BlockSpec((1, tile), lambda j: (0, j))],
        out_specs=pl.BlockSpec((bsz, tile), lambda j: (0, j)),
        out_shape=jax.ShapeDtypeStruct((bsz, cols), F32),
        compiler_params=_params("arbitrary"),
        name="mod",
    )(c, ada_w, ada_b.reshape(1, cols))


def _in_proj_kernel(x_ref, mod_ref, g_ref, w_ref, wkv_ref, f_ref, q_ref, kv_ref, ga_ref, gb_ref):
    u = _rms_mod(x_ref[...], g_ref[...], mod_ref[0, 1:2, :], mod_ref[0, 0:1, :]).astype(BF16)
    col = 0
    for out in (f_ref, q_ref, None, ga_ref, gb_ref):
        if out is None:
            kv_ref[...] = jnp.dot(u, wkv_ref[...], preferred_element_type=F32).astype(kv_ref.dtype)
            col += 2 * KV_DIM
            continue
        width = out.shape[1]
        out[...] = jnp.dot(u, w_ref[:, col:col + width], preferred_element_type=F32).astype(out.dtype)
        col += width


def _in_proj(x2, mod3, norm_g, w_in, w_kv, tiles_per_seq):
    rows = x2.shape[0]
    widths = (RWKV_COLS, ATTN_DIM, 4 * KV_DIM, D_MODEL, D_MODEL)
    dtypes = (F32, BF16, BF16, BF16, BF16)
    return pl.pallas_call(
        _in_proj_kernel,
        grid=(rows // ROW_TILE,),
        in_specs=[pl.BlockSpec((ROW_TILE, D_MODEL), lambda i: (i, 0)),
                  pl.BlockSpec((1, 6, D_MODEL), lambda i: (i // tiles_per_seq, 0, 0)),
                  _resident((1, D_MODEL)),
                  _resident(w_in.shape), _resident(w_kv.shape)],
        out_specs=[pl.BlockSpec((ROW_TILE, w), lambda i: (i, 0)) for w in widths],
        out_shape=[jax.ShapeDtypeStruct((rows, w), d) for w, d in zip(widths, dtypes)],
        compiler_params=_params("arbitrary"),
        name="in_proj",
    )(x2, mod3, norm_g.reshape(1, D_MODEL), w_in, w_kv)


def _prep_kernel(f_ref, prev_ref, mu_ref, lora_ref, gup_ref, vec_ref, ones_ref, perm_ref,
                 x_ref, pend_ref, g_ref, bonus_ref, group_ref):
    bsz, steps, cols = f_ref.shape
    feat = f_ref[...].reshape(ROW_TILE, cols)
    before = pltpu.roll(prev_ref[...].reshape(bsz * SUBLANES, cols), bsz * SUBLANES - (SUBLANES - 1), 0)
    before = jnp.where(pl.program_id(0) == 0, 0.0, before)
    before = _per_batch_rows(before.reshape(bsz, SUBLANES, cols), bsz)
    row = jax.lax.broadcasted_iota(jnp.int32, feat.shape, 0)
    shifted = jnp.where(row % steps == 0, before, pltpu.roll(feat, 1, 0))
    f = feat + (shifted - feat) * mu_ref[...]

    r = f[:, 0:RWKV_DIM]
    k = f[:, RWKV_DIM:2 * RWKV_DIM]
    v = f[:, 2 * RWKV_DIM:3 * RWKV_DIM]
    lo = f[:, 3 * RWKV_DIM:3 * RWKV_DIM + LORA_COLS]
    g_lo = f[:, 3 * RWKV_DIM + LORA_COLS:]
    w0, a0, k_k, k_a, r_k = (vec_ref[i:i + 1, :] for i in range(5))

    lane = jax.lax.broadcasted_iota(jnp.int32, lo.shape, 1)
    dot = functools.partial(jnp.dot, preferred_element_type=F32)
    lo_hi, lo_lo = _split(jnp.where(lane < DECAY_LORA, jnp.tanh(lo), lo))
    lora_hi, lora_lo = _split(lora_ref[...])
    lora = dot(lo_hi, lora_hi)
    lora_decay = (lora[:, :RWKV_DIM] + dot(lo_lo, lora_hi[:, :RWKV_DIM])) + dot(lo_hi, lora_lo[:, :RWKV_DIM])
    log_decay = -DECAY_SCALE * jax.nn.sigmoid(w0 + lora_decay)
    step_in_tile = jax.lax.broadcasted_iota(jnp.int32, (ROW_TILE, RWKV_DIM), 0) % steps
    cum = log_decay
    shift = 1
    while shift < steps:
        cum = cum + jnp.where(step_in_tile >= shift, pltpu.roll(cum, shift, 0), 0.0)
        shift *= 2
    decayed = jnp.exp(cum)
    grown = jnp.exp(-cum)
    decayed_before = jnp.exp(cum - log_decay)
    a = jax.nn.sigmoid(a0 + lora[:, RWKV_DIM:])
    g = _bdot(jax.nn.sigmoid(g_lo), gup_ref[...])

    ones = ones_ref[...]
    kk = k * k_k
    kk = kk / jnp.maximum(jnp.sqrt(_head_sum(kk * kk, ones)), 1e-12)
    k = k * (1.0 + (a - 1.0) * k_a)

    to_time = perm_ref[0]
    group_rows = bsz * SUBLANES
    groups = steps // SUBLANES

    def time_major(grp, pieces):
        return _permute_rows(to_time, group_ref[:, grp].reshape(group_rows, RWKV_DIM), pieces)

    scan_inputs = (r * decayed, k * grown, v, kk * decayed_before, kk * a * grown)
    for i, t in enumerate(scan_inputs):
        group_ref[...] = t.reshape(bsz, groups, SUBLANES, RWKV_DIM)
        for grp in range(groups):
            x_ref[i, grp * group_rows:(grp + 1) * group_rows, :] = time_major(grp, 2)
    group_ref[...] = decayed.reshape(bsz, groups, SUBLANES, RWKV_DIM)
    pend_ref[...] = time_major(groups - 1, 3)[(SUBLANES - 1) * bsz:]
    g_ref[...] = g.astype(BF16).reshape(bsz, steps, RWKV_DIM)
    bonus_ref[...] = (_bdot(r * k * r_k, ones) * v).astype(BF16).reshape(bsz, steps, RWKV_DIM)


def _prep(f_rwkv, mu, lora_w, g_up, vecs, ones_bd, perms):
    bsz, seq, cols = f_rwkv.shape
    steps = ROW_TILE // bsz
    batch_major = pl.BlockSpec((bsz, steps, RWKV_DIM), lambda i: (0, i, 0))
    return pl.pallas_call(
        _prep_kernel,
        grid=(seq // steps,),
        in_specs=[pl.BlockSpec((bsz, steps, cols), lambda i: (0, i, 0)),
                  pl.BlockSpec((bsz, SUBLANES, cols),
                               lambda i: (0, jnp.maximum(i * (steps // SUBLANES) - 1, 0), 0)),
                  _resident(mu.shape), _resident(lora_w.shape), _resident(g_up.shape),
                  _resident(vecs.shape), _resident(ones_bd.shape), _resident(perms.shape)],
        out_specs=[pl.BlockSpec((SCAN_INPUTS, ROW_TILE, RWKV_DIM), lambda i: (0, i, 0)),
                   pl.BlockSpec((bsz, RWKV_DIM), lambda i: (i, 0)),
                   batch_major, batch_major],
        out_shape=[jax.ShapeDtypeStruct((SCAN_INPUTS, seq * bsz, RWKV_DIM), F32),
                   jax.ShapeDtypeStruct((seq // steps * bsz, RWKV_DIM), F32)]
        + [jax.ShapeDtypeStruct((bsz, seq, RWKV_DIM), BF16)] * 2,
        scratch_shapes=[pltpu.VMEM((bsz, steps // SUBLANES, SUBLANES, RWKV_DIM), F32)],
        compiler_params=_params("arbitrary"),
        name="rwkv_prep",
    )(f_rwkv, f_rwkv, mu, lora_w, g_up, vecs, ones_bd, perms)


def _scan_kernel(x_ref, pend_ref, y_ref, state_ref, rows_even_ref, rows_odd_ref, yraw_ref):
    pairs = state_ref.shape[-1]
    bsz = x_ref.shape[2]
    two_heads = 2 * HEAD_DIM
    head_pairs = [slice(hp * two_heads, (hp + 1) * two_heads) for hp in range(RWKV_HEADS // 2)]
    R, K, V, KK, B = range(SCAN_INPUTS)
    per_group = HEAD_DIM // UPDATE_GROUPS
    in_update = (SCAN_INPUTS - 1) // UPDATE_GROUPS

    def regroup(slab_of):
        xt = jnp.concatenate([slab_of(lanes) for lanes in head_pairs], axis=0).T
        xt = jnp.concatenate([xt[:HEAD_DIM], xt[HEAD_DIM:]], axis=1)
        return xt.reshape(SUBLANES, SUBLANES, pairs)

    def transpose_in(i, t, rows_ref):
        rows_ref[i] = regroup(lambda lanes: x_ref[i, t, :, lanes])

    def write_out(t):
        y = yraw_ref[...]
        cen = y - jnp.mean(y, axis=0, keepdims=True)
        var = jnp.mean(cen * cen, axis=0, keepdims=True)
        yn = cen * jax.lax.rsqrt(var + GN_EPS)
        slab = jnp.concatenate([yn[:, :pairs // 2], yn[:, pairs // 2:]], axis=0).T
        for hp, lanes in enumerate(head_pairs):
            y_ref[t, :, lanes] = slab[hp * bsz:(hp + 1) * bsz]

    @pl.when(pl.program_id(0) == 0)
    def _():
        state_ref[...] = jnp.zeros_like(state_ref)
        yraw_ref[...] = jnp.zeros_like(yraw_ref)

    pend = regroup(lambda lanes: pend_ref[:, lanes]).reshape(HEAD_DIM, pairs)
    for j in range(HEAD_DIM):
        state_ref[j] = state_ref[j] * pend[j:j + 1, :]

    for i in range(SCAN_INPUTS):
        transpose_in(i, 0, rows_even_ref)

    def step(t, rows_ref, next_rows_ref):
        t_next = jnp.minimum(t + 1, SCAN_STEPS - 1)
        write_out(jnp.maximum(t - 1, 0))
        for i in range(in_update * UPDATE_GROUPS, SCAN_INPUTS):
            transpose_in(i, t_next, next_rows_ref)

        acc = jnp.zeros((HEAD_DIM, pairs), F32)
        for j in range(HEAD_DIM):
            acc = acc + state_ref[j] * rows_ref[KK, j // SUBLANES, j % SUBLANES:j % SUBLANES + 1, :]
        sa = -acc
        v = rows_ref[V].reshape(HEAD_DIM, pairs)

        def update(grp, y):
            for u in range(in_update):
                transpose_in(grp * in_update + u, t_next, next_rows_ref)
            for jj in range(per_group):
                j = grp * per_group + jj
                jh = grp * (per_group // SUBLANES) + jj // SUBLANES
                kl = slice(jj % SUBLANES, jj % SUBLANES + 1)
                s_new = state_ref[j] + sa * rows_ref[B, jh, kl, :] + v * rows_ref[K, jh, kl, :]
                state_ref[j] = s_new
                y = y + s_new * rows_ref[R, jh, kl, :]
            return y

        yraw_ref[...] = jax.lax.fori_loop(0, UPDATE_GROUPS, update, jnp.zeros((HEAD_DIM, pairs), F32))

    def two_steps(i, carry):
        step(2 * i, rows_even_ref, rows_odd_ref)
        step(2 * i + 1, rows_odd_ref, rows_even_ref)
        return carry

    jax.lax.fori_loop(0, SCAN_STEPS // 2, two_steps, 0)
    write_out(SCAN_STEPS - 1)


def _scan(x, pend):
    _, seq, bsz, _ = x.shape
    pairs = bsz * RWKV_HEADS
    return pl.pallas_call(
        _scan_kernel,
        grid=(seq // SCAN_STEPS,),
        in_specs=[pl.BlockSpec((SCAN_INPUTS, SCAN_STEPS, bsz, RWKV_DIM), lambda i: (0, i, 0, 0)),
                  pl.BlockSpec((bsz, RWKV_DIM), lambda i: (jnp.maximum(i - 1, 0), 0))],
        out_specs=pl.BlockSpec((SCAN_STEPS, bsz, RWKV_DIM), lambda i: (i, 0, 0)),
        out_shape=jax.ShapeDtypeStruct((seq, bsz, RWKV_DIM), F32),
        scratch_shapes=[pltpu.VMEM((HEAD_DIM, HEAD_DIM, pairs), F32),
                        pltpu.VMEM((SCAN_INPUTS, SUBLANES, SUBLANES, pairs), F32),
                        pltpu.VMEM((SCAN_INPUTS, SUBLANES, SUBLANES, pairs), F32),
                        pltpu.VMEM((HEAD_DIM, pairs), F32)],
        compiler_params=_params("arbitrary"),
        name="rwkv_scan",
    )(x, pend)


def _attn_kernel(q_ref, kv_ref, kvp_ref, bias_first_ref, bias_ref, sink_ref, o_ref):
    kv_all = jnp.concatenate([kvp_ref[...], kv_ref[...]], axis=0)
    for blk in range(ATTN_BLOCKS_PER_STEP):
        rows = slice(blk * ATTN_BLOCK, (blk + 1) * ATTN_BLOCK)
        _attend_block(q_ref.at[rows], kv_all[blk * ATTN_BLOCK:(blk + 2) * ATTN_BLOCK],
                      bias_first_ref if blk == 0 else bias_ref, sink_ref, o_ref.at[rows])


def _attend_block(q_ref, kv, bias_ref, sink_ref, o_ref):
    low = jax.lax.broadcasted_iota(jnp.int32, (2 * ATTN_BLOCK, 2 * HEAD_DIM), 1) < HEAD_DIM
    low_out = jax.lax.broadcasted_iota(jnp.int32, (ATTN_BLOCK, 2 * HEAD_DIM), 1) < HEAD_DIM
    nothing = jnp.zeros((2 * ATTN_BLOCK, 2 * HEAD_DIM), BF16)

    def block_diag(t):
        return jnp.concatenate([jnp.where(low, t, nothing), jnp.where(low, nothing, t)], axis=0)

    pairs = range(ATTN_HEADS // 2)
    pairs_per_kv = ATTN_GROUP // 2
    lanes = [slice(n * 2 * HEAD_DIM, (n + 1) * 2 * HEAD_DIM) for n in range(2 * ATTN_KV_HEADS)]
    k_pairs = [block_diag(kv[:, lanes[kvh]]) for kvh in range(ATTN_KV_HEADS)]
    v_pairs = [block_diag(kv[:, lanes[ATTN_KV_HEADS + kvh]]) for kvh in range(ATTN_KV_HEADS)]
    scores = []
    for pair in pairs:
        q2 = q_ref[:, lanes[pair]] * (HEAD_DIM ** -0.5)
        s = jax.lax.dot_general(q2, k_pairs[pair // pairs_per_kv], (((1,), (1,)), ((), ())),
                                preferred_element_type=F32)
        scores.append(s + bias_ref[0, pair])
    probs, scales = [], []
    for pair in pairs:
        halves, inv = [], []
        for half in range(2):
            sh = scores[pair][:, half * 2 * ATTN_BLOCK:(half + 1) * 2 * ATTN_BLOCK]
            sink = sink_ref[2 * pair + half]
            m = jnp.maximum(jnp.max(sh, axis=-1, keepdims=True), sink)
            p = jnp.exp(sh - m)
            inv.append(1.0 / (jnp.sum(p, axis=-1, keepdims=True) + jnp.exp(sink - m)))
            halves.append(p.astype(BF16))
        probs.append(jnp.concatenate(halves, axis=1))
        scales.append(jnp.where(low_out, inv[0], inv[1]))
    for pair in pairs:
        o = jnp.dot(probs[pair], v_pairs[pair // pairs_per_kv], preferred_element_type=F32)
        o_ref[:, lanes[pair]] = (o * scales[pair]).astype(o_ref.dtype)


def _attn_bias():
    qi = jnp.arange(ATTN_BLOCK)[:, None]
    sj = jnp.arange(2 * ATTN_BLOCK)[None, :]
    dist = qi + ATTN_BLOCK - sj
    in_window = (dist >= 0) & (dist < ATTN_BLOCK)
    slopes = 2.0 ** (-8.0 * jnp.arange(1, ATTN_HEADS + 1, dtype=F32) / ATTN_HEADS)
    alibi = -slopes[:, None, None] * dist.astype(F32)
    valid = jnp.stack([in_window & (sj >= ATTN_BLOCK), in_window])[:, None]
    bias = jnp.where(valid, alibi[None], NEG_BIG)
    bias = bias.reshape(2, ATTN_HEADS // 2, 2, ATTN_BLOCK, 2 * ATTN_BLOCK).transpose(0, 1, 3, 2, 4)
    return bias.reshape(2, ATTN_HEADS // 2, ATTN_BLOCK, 4 * ATTN_BLOCK)


def _attn(q, kv, sinks, blocks_per_seq):
    rows = q.shape[0]
    bias = _attn_bias()
    per_step = ATTN_BLOCKS_PER_STEP
    steps_per_seq = blocks_per_seq // per_step
    step_rows = per_step * ATTN_BLOCK
    return pl.pallas_call(
        _attn_kernel,
        grid=(rows // step_rows // steps_per_seq, steps_per_seq),
        in_specs=[pl.BlockSpec((step_rows, ATTN_DIM), lambda b, i: (b * steps_per_seq + i, 0)),
                  pl.BlockSpec((step_rows, 4 * KV_DIM), lambda b, i: (b * steps_per_seq + i, 0)),
                  pl.BlockSpec((ATTN_BLOCK, 4 * KV_DIM),
                               lambda b, i: (b * blocks_per_seq + jnp.maximum(i * per_step - 1, 0), 0)),
                  pl.BlockSpec((1,) + bias.shape[1:], lambda b, i: (jnp.minimum(i, 1), 0, 0, 0)),
                  pl.BlockSpec((1,) + bias.shape[1:], lambda b, i: (1, 0, 0, 0)),
                  pl.BlockSpec(memory_space=pltpu.SMEM)],
        out_specs=pl.BlockSpec((step_rows, ATTN_DIM), lambda b, i: (b * steps_per_seq + i, 0)),
        out_shape=jax.ShapeDtypeStruct((rows, ATTN_DIM), BF16),
        compiler_params=_params("arbitrary", "arbitrary"),
        name="swa_attn",
    )(q, kv, kv, bias, bias, sinks)


def _merge_kernel(x_ref, mod_ref, yn_ref, bonus_ref, g_ref, yb_ref, ga_ref, gb_ref, gn_ref, n2_ref,
                  wa_ref, wb_ref, wo_ref, perm_ref, h_ref, u_ref, group_ref):
    bsz, steps, _ = x_ref.shape
    group_rows = bsz * SUBLANES
    to_batch = perm_ref[1]
    for grp in range(steps // SUBLANES):
        rows = yn_ref[grp * SUBLANES:(grp + 1) * SUBLANES].reshape(group_rows, RWKV_DIM)
        group_ref[:, grp] = _permute_rows(to_batch, rows).reshape(bsz, SUBLANES, RWKV_DIM)
    yn = group_ref[...].reshape(ROW_TILE, RWKV_DIM)

    def tile(ref):
        return ref[...].reshape(ROW_TILE, ref.shape[-1])

    ya = ((yn * gn_ref[0:1, :] + gn_ref[1:2, :]) + tile(bonus_ref).astype(F32)) * tile(g_ref).astype(F32)
    pa = _bdot(ya, wa_ref[...])
    pb = jnp.dot(tile(yb_ref), wb_ref[...], preferred_element_type=F32)
    merged = (jax.nn.sigmoid(tile(ga_ref).astype(F32)) * pa
              + jax.nn.sigmoid(tile(gb_ref).astype(F32)) * pb)
    h = tile(x_ref) + _per_batch_rows(mod_ref[2], bsz) * _bdot(merged, wo_ref[...])
    h_ref[...] = h.reshape(h_ref.shape)
    u = _rms_mod(h, n2_ref[...], _per_batch_rows(mod_ref[4], bsz), _per_batch_rows(mod_ref[3], bsz))
    u_ref[...] = u.astype(BF16).reshape(u_ref.shape)


def _merge(x3, mod_rows, yn, bonus, g, yb, ga, gb, gn, norm2_g, wa, wb, wo, perms):
    bsz, seq, _ = x3.shape
    steps = ROW_TILE // bsz

    def tile(width):
        return pl.BlockSpec((bsz, steps, width), lambda i: (0, i, 0))

    return pl.pallas_call(
        _merge_kernel,
        grid=(seq // steps,),
        in_specs=[tile(D_MODEL), _resident(mod_rows.shape),
                  pl.BlockSpec((steps, bsz, RWKV_DIM), lambda i: (i, 0, 0)),
                  tile(RWKV_DIM), tile(RWKV_DIM), tile(ATTN_DIM), tile(D_MODEL), tile(D_MODEL),
                  _resident(gn.shape), _resident((1, D_MODEL)),
                  _resident(wa.shape), _resident(wb.shape), _resident(wo.shape), _resident(perms.shape)],
        out_specs=[tile(D_MODEL), tile(D_MODEL)],
        out_shape=[jax.ShapeDtypeStruct((bsz, seq, D_MODEL), F32),
                   jax.ShapeDtypeStruct((bsz, seq, D_MODEL), BF16)],
        scratch_shapes=[pltpu.VMEM((bsz, steps // SUBLANES, SUBLANES, RWKV_DIM), F32)],
        compiler_params=_params("arbitrary"),
        name="merge",
    )(x3, mod_rows, yn, bonus, g, yb, ga, gb, gn, norm2_g.reshape(1, D_MODEL), wa, wb, wo, perms)


def _ffn_kernel(tiles_per_seq, h_ref, u_ref, mod_ref, wu_ref, cw_ref, cb_ref, wd_ref, fg_ref,
                o_ref, carry_ref, act_ref):
    @pl.when(pl.program_id(0) % tiles_per_seq == 0)
    def _():
        carry_ref[...] = jnp.zeros_like(carry_ref)

    u = u_ref[...]
    row = jax.lax.broadcasted_iota(jnp.int32, (ROW_TILE, FFN_CHUNK), 0)
    chunks = [slice(c * FFN_CHUNK, (c + 1) * FFN_CHUNK) for c in range(D_FF // FFN_CHUNK)]

    def up(cols):
        return (jnp.dot(u, wu_ref[:, cols], preferred_element_type=F32),
                jnp.dot(u, wu_ref[:, D_FF + cols.start:D_FF + cols.stop], preferred_element_type=F32))

    ahead = [up(cols) for cols in chunks[:FFN_LOOKAHEAD]]
    for c, cols in enumerate(chunks):
        gate, val = ahead.pop(0)
        if c + FFN_LOOKAHEAD < len(chunks):
            ahead.append(up(chunks[c + FFN_LOOKAHEAD]))
        back1 = carry_ref[SUBLANES - 1:SUBLANES, cols]
        back2 = carry_ref[SUBLANES - 2:SUBLANES - 1, cols]
        carry_ref[:, cols] = gate[ROW_TILE - SUBLANES:, :]
        shift1 = jnp.where(row == 0, back1, pltpu.roll(gate, 1, 0))
        shift2 = jnp.where(row == 0, back2, jnp.where(row == 1, back1, pltpu.roll(gate, 2, 0)))
        conv = (cw_ref[0:1, cols] * shift2 + cw_ref[1:2, cols] * shift1 + cw_ref[2:3, cols] * gate
                + cb_ref[:, cols])
        act_ref[:, cols] = (conv * jax.nn.sigmoid(conv) * val).astype(BF16)
    acc = jnp.dot(act_ref[...], wd_ref[...], preferred_element_type=F32)
    h = h_ref[...] + mod_ref[0, 5:6, :] * acc
    o_ref[...] = h * jax.lax.rsqrt(jnp.mean(h * h, axis=-1, keepdims=True) + RMS_EPS) * fg_ref[...]


def _ffn(h1, u2, mod3, w_up, conv_w, conv_b, w_down, final_g, tiles_per_seq):
    rows = h1.shape[0]
    return pl.pallas_call(
        functools.partial(_ffn_kernel, tiles_per_seq),
        grid=(rows // ROW_TILE,),
        in_specs=[pl.BlockSpec((ROW_TILE, D_MODEL), lambda i: (i, 0)),
                  pl.BlockSpec((ROW_TILE, D_MODEL), lambda i: (i, 0)),
                  pl.BlockSpec((1, 6, D_MODEL), lambda i: (i // tiles_per_seq, 0, 0)),
                  _resident(w_up.shape), _resident(conv_w.shape),
                  _resident((1, D_FF)), _resident(w_down.shape), _resident((1, D_MODEL))],
        out_specs=pl.BlockSpec((ROW_TILE, D_MODEL), lambda i: (i, 0)),
        out_shape=jax.ShapeDtypeStruct((rows, D_MODEL), F32),
        scratch_shapes=[pltpu.VMEM((SUBLANES, D_FF), F32), pltpu.VMEM((ROW_TILE, D_FF), BF16)],
        compiler_params=_params("arbitrary"),
        name="ffn",
    )(h1, u2, mod3, w_up, conv_w, conv_b.reshape(1, D_FF), w_down, final_g.reshape(1, D_MODEL))


def kernel(x, c, ada_w, ada_b, norm1_g, w_in, rwkv_mu, rwkv_w0, rwkv_w_up, rwkv_a0, rwkv_a_up, rwkv_g_up, rwkv_k_k, rwkv_k_a, rwkv_r_k, rwkv_gn_w, rwkv_gn_b, attn_sinks, w_branch_a, w_branch_b, w_out, norm2_g, ffn_w_up, ffn_conv_w, ffn_conv_b, ffn_w_down, final_g):
    bsz, seq, _ = x.shape
    depth = ada_w.shape[0]
    rows = bsz * seq
    assert depth == 1, "the ffn kernel fuses the final norm, which follows the last layer only"
    assert bsz * RWKV_HEADS == 128 and seq % ROW_TILE == 0 and seq % SCAN_STEPS == 0
    assert ROW_TILE == bsz * SCAN_STEPS, "prep restarts the decay product per tile, the scan per block"
    tiles_per_seq = seq // ROW_TILE
    ones_bd = jnp.kron(jnp.eye(RWKV_HEADS, dtype=F32), jnp.ones((HEAD_DIM, HEAD_DIM), F32)).astype(BF16)
    perms = _row_order_perms(bsz)
    h = x.reshape(rows, D_MODEL)
    for l in range(depth):
        mod3 = _mod(c, ada_w[l], ada_b[l]).reshape(bsz, 6, D_MODEL)
        mod_rows = jnp.broadcast_to(mod3.transpose(1, 0, 2)[:, :, None, :], (6, bsz, SUBLANES, D_MODEL))
        kv0 = RWKV_COLS + ATTN_DIM
        kv_cols = kv0 + (jnp.arange(4 * KV_DIM) // (2 * HEAD_DIM)) * HEAD_DIM + jnp.arange(4 * KV_DIM) % HEAD_DIM
        f_rwkv, q, kv, gate_a, gate_b = _in_proj(h, mod3, norm1_g[l], w_in[l].astype(BF16),
                                                 w_in[l][:, kv_cols].astype(BF16), tiles_per_seq)

        lora_w = jnp.zeros((LORA_COLS, 2 * RWKV_DIM), F32)
        lora_w = lora_w.at[:DECAY_LORA, :RWKV_DIM].set(rwkv_w_up[l]).at[DECAY_LORA:, RWKV_DIM:].set(rwkv_a_up[l])
        vecs = jnp.stack([rwkv_w0[l], rwkv_a0[l], rwkv_k_k[l], rwkv_k_a[l], rwkv_r_k[l].reshape(RWKV_DIM)])
        scan_in, pend, g, bonus = _prep(f_rwkv.reshape(bsz, seq, RWKV_COLS), rwkv_mu[l].reshape(1, RWKV_COLS),
                                        lora_w, rwkv_g_up[l], vecs, ones_bd, perms)
        yn = _scan(scan_in.reshape(SCAN_INPUTS, seq, bsz, RWKV_DIM), pend)

        yb = _attn(q, kv, attn_sinks[l], seq // ATTN_BLOCK)

        gn = jnp.stack([rwkv_gn_w[l], rwkv_gn_b[l]])
        by_batch = lambda t: t.reshape(bsz, seq, t.shape[-1])
        h3, u2 = _merge(by_batch(h), mod_rows, yn, bonus, g, by_batch(yb), by_batch(gate_a), by_batch(gate_b),
                        gn, norm2_g[l], w_branch_a[l].astype(BF16), w_branch_b[l].astype(BF16),
                        w_out[l].astype(BF16), perms)
        w_up = ffn_w_up[l].astype(BF16)
        h = _ffn(h3.reshape(rows, D_MODEL), u2.reshape(rows, D_MODEL), mod3, w_up,
                 ffn_conv_w[l], ffn_conv_b[l], ffn_w_down[l].astype(BF16), final_g, tiles_per_seq)
    return h.reshape(bsz, seq, D_MODEL)
```

```python
import functools

import jax
import jax.numpy as jnp
from jax.experimental import pallas as pl
from jax.experimental.pallas import tpu as pltpu

F32 = jnp.float32
BF16 = jnp.bfloat16

D_MODEL = 1024
RWKV_HEADS = 8
HEAD_DIM = 64
RWKV_DIM = RWKV_HEADS * HEAD_DIM
DECAY_LORA = 64
AAA_LORA = 64
GATE_LORA = 128
LORA_COLS = DECAY_LORA + AAA_LORA
RWKV_COLS = 3 * RWKV_DIM + LORA_COLS + GATE_LORA
ATTN_HEADS = 8
ATTN_KV_HEADS = 2
ATTN_GROUP = ATTN_HEADS // ATTN_KV_HEADS
ATTN_DIM = ATTN_HEADS * HEAD_DIM
KV_DIM = ATTN_KV_HEADS * HEAD_DIM
ATTN_BLOCK = 128
ATTN_BLOCKS_PER_STEP = 4
D_FF = 2816
GN_EPS = 64e-5
RMS_EPS = 1e-6
DECAY_SCALE = 0.6065306597126334

VMEM_LIMIT_BYTES = 56 * 1024 * 1024
ROW_TILE = 512
FFN_CHUNK = 256
FFN_LOOKAHEAD = 2
SCAN_STEPS = 32
SCAN_INPUTS = 5
UPDATE_GROUPS = 2
SUBLANES = 8
NEG_BIG = -1e30


def _params(*semantics):
    return pltpu.CompilerParams(dimension_semantics=semantics, vmem_limit_bytes=VMEM_LIMIT_BYTES)


def _resident(shape):
    zeros = (0,) * len(shape)
    return pl.BlockSpec(shape, lambda *_: zeros, pipeline_mode=pl.Buffered(1))


def _bdot(a, b):
    return jnp.dot(a.astype(BF16), b.astype(BF16), preferred_element_type=F32)


def _split(x):
    hi = x.astype(BF16)
    lo = (x - hi.astype(F32)).astype(BF16)
    return hi, lo


def _head_sum(x, ones_blockdiag):
    hi, lo = _split(x)
    dot = functools.partial(jnp.dot, preferred_element_type=F32)
    return dot(hi, ones_blockdiag) + dot(lo, ones_blockdiag)


def _rms_mod(h, gain, scale, shift):
    y = h * jax.lax.rsqrt(jnp.mean(h * h, axis=-1, keepdims=True) + RMS_EPS)
    return y * (gain * (1.0 + scale)) + shift


def _permute_rows(p, x, pieces=3):
    dot = functools.partial(jnp.dot, preferred_element_type=F32)
    out = None
    for _ in range(pieces):
        piece = x.astype(BF16)
        x = x - piece.astype(F32)
        out = dot(p, piece) if out is None else out + dot(p, piece)
    return out


def _row_order_perms(bsz):
    rows = bsz * SUBLANES
    src = jnp.arange(rows)[None, :]
    to_time = jnp.arange(rows)[:, None] == (src % SUBLANES) * bsz + src // SUBLANES
    return jnp.stack([to_time, to_time.T]).astype(BF16)


def _per_batch_rows(t, bsz):
    groups = ROW_TILE // bsz // SUBLANES
    return jnp.broadcast_to(t[:, None], (bsz, groups) + t.shape[1:]).reshape(ROW_TILE, t.shape[-1])


def _mod_kernel(c_ref, w_ref, b_ref, o_ref):
    c = c_ref[...]
    o_ref[...] = _bdot(c * jax.nn.sigmoid(c), w_ref[...]) + b_ref[...]


def _mod(c, ada_w, ada_b, layer):
    bsz = c.shape[0]
    cols = ada_w.shape[2]
    tile = cols // 4
    return pl.pallas_call(
        _mod_kernel,
        grid=(cols // tile,),
        in_specs=[pl.BlockSpec((bsz, D_MODEL), lambda j: (0, 0)),
                  pl.BlockSpec((None, D_MODEL, tile), lambda j: (layer, 0, j)),
                  pl.BlockSpec((1, tile), lambda j: (0, j))],
        out_specs=pl.BlockSpec((bsz, tile), lambda j: (0, j)),
        out_shape=jax.ShapeDtypeStruct((bsz, cols), F32),
        compiler_params=_params("arbitrary"),
        name="mod",
    )(c, ada_w, ada_b.reshape(1, cols))


def _in_proj_kernel(x_ref, mod_ref, g_ref, w_ref, wkv_ref, f_ref, q_ref, kv_ref, ga_ref, gb_ref):
    u = _rms_mod(x_ref[...], g_ref[...], mod_ref[0, 1:2, :], mod_ref[0, 0:1, :]).astype(BF16)
    col = 0
    for out in (f_ref, q_ref, None, ga_ref, gb_ref):
        if out is None:
            kv_ref[...] = jnp.dot(u, wkv_ref[...], preferred_element_type=F32).astype(kv_ref.dtype)
            col += 2 * KV_DIM
            continue
        width = out.shape[1]
        out[...] = jnp.dot(u, w_ref[:, col:col + width], preferred_element_type=F32).astype(out.dtype)
        col += width


def _in_proj(x2, mod3, norm_g, w_in, w_kv, tiles_per_seq):
    rows = x2.shape[0]
    widths = (RWKV_COLS, ATTN_DIM, 4 * KV_DIM, D_MODEL, D_MODEL)
    dtypes = (F32, BF16, BF16, BF16, BF16)
    return pl.pallas_call(
        _in_proj_kernel,
        grid=(rows // ROW_TILE,),
        in_specs=[pl.BlockSpec((ROW_TILE, D_MODEL), lambda i: (i, 0)),
                  pl.BlockSpec((1, 6, D_MODEL), lambda i: (i // tiles_per_seq, 0, 0)),
                  _resident((1, D_MODEL)),
                  _resident(w_in.shape), _resident(w_kv.shape)],
        out_specs=[pl.BlockSpec((ROW_TILE, w), lambda i: (i, 0)) for w in widths],
        out_shape=[jax.ShapeDtypeStruct((rows, w), d) for w, d in zip(widths, dtypes)],
        compiler_params=_params("arbitrary"),
        name="in_proj",
    )(x2, mod3, norm_g.reshape(1, D_MODEL), w_in, w_kv)


def _prep_kernel(f_ref, prev_ref, mu_ref, lora_ref, gup_ref, vec_ref, ones_ref, perm_ref,
                 x_ref, pend_ref, g_ref, bonus_ref, group_ref):
    bsz, steps, cols = f_ref.shape
    feat = f_ref[...].reshape(ROW_TILE, cols)
    before = pltpu.roll(prev_ref[...].reshape(bsz * SUBLANES, cols), bsz * SUBLANES - (SUBLANES - 1), 0)
    before = jnp.where(pl.program_id(0) == 0, 0.0, before)
    before = _per_batch_rows(before.reshape(bsz, SUBLANES, cols), bsz)
    row = jax.lax.broadcasted_iota(jnp.int32, feat.shape, 0)
    shifted = jnp.where(row % steps == 0, before, pltpu.roll(feat, 1, 0))
    f = feat + (shifted - feat) * mu_ref[...]

    r = f[:, 0:RWKV_DIM]
    k = f[:, RWKV_DIM:2 * RWKV_DIM]
    v = f[:, 2 * RWKV_DIM:3 * RWKV_DIM]
    lo = f[:, 3 * RWKV_DIM:3 * RWKV_DIM + LORA_COLS]
    g_lo = f[:, 3 * RWKV_DIM + LORA_COLS:]
    w0, a0, k_k, k_a, r_k = (vec_ref[i:i + 1, :] for i in range(5))

    lane = jax.lax.broadcasted_iota(jnp.int32, lo.shape, 1)
    dot = functools.partial(jnp.dot, preferred_element_type=F32)
    lo_hi, lo_lo = _split(jnp.where(lane < DECAY_LORA, jnp.tanh(lo), lo))
    lora_hi, lora_lo = _split(lora_ref[...])
    lora = dot(lo_hi, lora_hi)
    lora_decay = (lora[:, :RWKV_DIM] + dot(lo_lo, lora_hi[:, :RWKV_DIM])) + dot(lo_hi, lora_lo[:, :RWKV_DIM])
    log_decay = -DECAY_SCALE * jax.nn.sigmoid(w0 + lora_decay)
    step_in_tile = jax.lax.broadcasted_iota(jnp.int32, (ROW_TILE, RWKV_DIM), 0) % steps
    cum = log_decay
    shift = 1
    while shift < steps:
        cum = cum + jnp.where(step_in_tile >= shift, pltpu.roll(cum, shift, 0), 0.0)
        shift *= 2
    decayed = jnp.exp(cum)
    grown = jnp.exp(-cum)
    decayed_before = jnp.exp(cum - log_decay)
    a = jax.nn.sigmoid(a0 + lora[:, RWKV_DIM:])
    g = _bdot(jax.nn.sigmoid(g_lo), gup_ref[...])

    ones = ones_ref[...]
    kk = k * k_k
    kk = kk / jnp.maximum(jnp.sqrt(_head_sum(kk * kk, ones)), 1e-12)
    k = k * (1.0 + (a - 1.0) * k_a)

    to_time = perm_ref[0]
    group_rows = bsz * SUBLANES
    groups = steps // SUBLANES

    def time_major(grp, pieces):
        return _permute_rows(to_time, group_ref[:, grp].reshape(group_rows, RWKV_DIM), pieces)

    scan_inputs = (r * decayed, k * grown, v, kk * decayed_before, kk * a * grown)
    for i, t in enumerate(scan_inputs):
        group_ref[...] = t.reshape(bsz, groups, SUBLANES, RWKV_DIM)
        for grp in range(groups):
            x_ref[i, grp * group_rows:(grp + 1) * group_rows, :] = time_major(grp, 2)
    group_ref[...] = decayed.reshape(bsz, groups, SUBLANES, RWKV_DIM)
    pend_ref[...] = time_major(groups - 1, 3)[(SUBLANES - 1) * bsz:]
    g_ref[...] = g.astype(BF16).reshape(bsz, steps, RWKV_DIM)
    bonus_ref[...] = (_bdot(r * k * r_k, ones) * v).astype(BF16).reshape(bsz, steps, RWKV_DIM)


def _prep(f_rwkv, mu, lora_w, g_up, vecs, ones_bd, perms):
    bsz, seq, cols = f_rwkv.shape
    steps = ROW_TILE // bsz
    batch_major = pl.BlockSpec((bsz, steps, RWKV_DIM), lambda i: (0, i, 0))
    return pl.pallas_call(
        _prep_kernel,
        grid=(seq // steps,),
        in_specs=[pl.BlockSpec((bsz, steps, cols), lambda i: (0, i, 0)),
                  pl.BlockSpec((bsz, SUBLANES, cols),
                               lambda i: (0, jnp.maximum(i * (steps // SUBLANES) - 1, 0), 0)),
                  _resident(mu.shape), _resident(lora_w.shape), _resident(g_up.shape),
                  _resident(vecs.shape), _resident(ones_bd.shape), _resident(perms.shape)],
        out_specs=[pl.BlockSpec((SCAN_INPUTS, ROW_TILE, RWKV_DIM), lambda i: (0, i, 0)),
                   pl.BlockSpec((bsz, RWKV_DIM), lambda i: (i, 0)),
                   batch_major, batch_major],
        out_shape=[jax.ShapeDtypeStruct((SCAN_INPUTS, seq * bsz, RWKV_DIM), F32),
                   jax.ShapeDtypeStruct((seq // steps * bsz, RWKV_DIM), F32)]
        + [jax.ShapeDtypeStruct((bsz, seq, RWKV_DIM), BF16)] * 2,
        scratch_shapes=[pltpu.VMEM((bsz, steps // SUBLANES, SUBLANES, RWKV_DIM), F32)],
        compiler_params=_params("arbitrary"),
        name="rwkv_prep",
    )(f_rwkv, f_rwkv, mu, lora_w, g_up, vecs, ones_bd, perms)


def _scan_kernel(x_ref, pend_ref, y_ref, state_ref, rows_even_ref, rows_odd_ref, yraw_ref):
    pairs = state_ref.shape[-1]
    bsz = x_ref.shape[2]
    two_heads = 2 * HEAD_DIM
    head_pairs = [slice(hp * two_heads, (hp + 1) * two_heads) for hp in range(RWKV_HEADS // 2)]
    R, K, V, KK, B = range(SCAN_INPUTS)
    per_group = HEAD_DIM // UPDATE_GROUPS
    in_update = (SCAN_INPUTS - 1) // UPDATE_GROUPS

    def regroup(slab_of):
        xt = jnp.concatenate([slab_of(lanes) for lanes in head_pairs], axis=0).T
        xt = jnp.concatenate([xt[:HEAD_DIM], xt[HEAD_DIM:]], axis=1)
        return xt.reshape(SUBLANES, SUBLANES, pairs)

    def transpose_in(i, t, rows_ref):
        rows_ref[i] = regroup(lambda lanes: x_ref[i, t, :, lanes])

    def write_out(t):
        y = yraw_ref[...]
        cen = y - jnp.mean(y, axis=0, keepdims=True)
        var = jnp.mean(cen * cen, axis=0, keepdims=True)
        yn = cen * jax.lax.rsqrt(var + GN_EPS)
        slab = jnp.concatenate([yn[:, :pairs // 2], yn[:, pairs // 2:]], axis=0).T
        for hp, lanes in enumerate(head_pairs):
            y_ref[t, :, lanes] = slab[hp * bsz:(hp + 1) * bsz]

    @pl.when(pl.program_id(0) == 0)
    def _():
        state_ref[...] = jnp.zeros_like(state_ref)
        yraw_ref[...] = jnp.zeros_like(yraw_ref)

    pend = regroup(lambda lanes: pend_ref[:, lanes]).reshape(HEAD_DIM, pairs)
    for j in range(HEAD_DIM):
        state_ref[j] = state_ref[j] * pend[j:j + 1, :]

    for i in range(SCAN_INPUTS):
        transpose_in(i, 0, rows_even_ref)

    def step(t, rows_ref, next_rows_ref):
        t_next = jnp.minimum(t + 1, SCAN_STEPS - 1)
        write_out(jnp.maximum(t - 1, 0))
        for i in range(in_update * UPDATE_GROUPS, SCAN_INPUTS):
            transpose_in(i, t_next, next_rows_ref)

        acc = jnp.zeros((HEAD_DIM, pairs), F32)
        for j in range(HEAD_DIM):
            acc = acc + state_ref[j] * rows_ref[KK, j // SUBLANES, j % SUBLANES:j % SUBLANES + 1, :]
        sa = -acc
        v = rows_ref[V].reshape(HEAD_DIM, pairs)

        def update(grp, y):
            for u in range(in_update):
                transpose_in(grp * in_update + u, t_next, next_rows_ref)
            for jj in range(per_group):
                j = grp * per_group + jj
                jh = grp * (per_group // SUBLANES) + jj // SUBLANES
                kl = slice(jj % SUBLANES, jj % SUBLANES + 1)
                s_new = state_ref[j] + sa * rows_ref[B, jh, kl, :] + v * rows_ref[K, jh, kl, :]
                state_ref[j] = s_new
                y = y + s_new * rows_ref[R, jh, kl, :]
            return y

        yraw_ref[...] = jax.lax.fori_loop(0, UPDATE_GROUPS, update, jnp.zeros((HEAD_DIM, pairs), F32))

    def two_steps(i, carry):
        step(2 * i, rows_even_ref, rows_odd_ref)
        step(2 * i + 1, rows_odd_ref, rows_even_ref)
        return carry

    jax.lax.fori_loop(0, SCAN_STEPS // 2, two_steps, 0)
    write_out(SCAN_STEPS - 1)


def _scan(x, pend):
    _, seq, bsz, _ = x.shape
    pairs = bsz * RWKV_HEADS
    return pl.pallas_call(
        _scan_kernel,
        grid=(seq // SCAN_STEPS,),
        in_specs=[pl.BlockSpec((SCAN_INPUTS, SCAN_STEPS, bsz, RWKV_DIM), lambda i: (0, i, 0, 0)),
                  pl.BlockSpec((bsz, RWKV_DIM), lambda i: (jnp.maximum(i - 1, 0), 0))],
        out_specs=pl.BlockSpec((SCAN_STEPS, bsz, RWKV_DIM), lambda i: (i, 0, 0)),
        out_shape=jax.ShapeDtypeStruct((seq, bsz, RWKV_DIM), F32),
        scratch_shapes=[pltpu.VMEM((HEAD_DIM, HEAD_DIM, pairs), F32),
                        pltpu.VMEM((SCAN_INPUTS, SUBLANES, SUBLANES, pairs), F32),
                        pltpu.VMEM((SCAN_INPUTS, SUBLANES, SUBLANES, pairs), F32),
                        pltpu.VMEM((HEAD_DIM, pairs), F32)],
        compiler_params=_params("arbitrary"),
        name="rwkv_scan",
    )(x, pend)


def _attn_kernel(q_ref, kv_ref, kvp_ref, bias_first_ref, bias_ref, sink_ref, o_ref):
    kv_all = jnp.concatenate([kvp_ref[...], kv_ref[...]], axis=0)
    for blk in range(ATTN_BLOCKS_PER_STEP):
        rows = slice(blk * ATTN_BLOCK, (blk + 1) * ATTN_BLOCK)
        _attend_block(q_ref.at[rows], kv_all[blk * ATTN_BLOCK:(blk + 2) * ATTN_BLOCK],
                      bias_first_ref if blk == 0 else bias_ref, sink_ref, o_ref.at[rows])


def _attend_block(q_ref, kv, bias_ref, sink_ref, o_ref):
    low = jax.lax.broadcasted_iota(jnp.int32, (2 * ATTN_BLOCK, 2 * HEAD_DIM), 1) < HEAD_DIM
    low_out = jax.lax.broadcasted_iota(jnp.int32, (ATTN_BLOCK, 2 * HEAD_DIM), 1) < HEAD_DIM
    nothing = jnp.zeros((2 * ATTN_BLOCK, 2 * HEAD_DIM), BF16)

    def block_diag(t):
        return jnp.concatenate([jnp.where(low, t, nothing), jnp.where(low, nothing, t)], axis=0)

    pairs = range(ATTN_HEADS // 2)
    pairs_per_kv = ATTN_GROUP // 2
    lanes = [slice(n * 2 * HEAD_DIM, (n + 1) * 2 * HEAD_DIM) for n in range(2 * ATTN_KV_HEADS)]
    k_pairs = [block_diag(kv[:, lanes[kvh]]) for kvh in range(ATTN_KV_HEADS)]
    v_pairs = [block_diag(kv[:, lanes[ATTN_KV_HEADS + kvh]]) for kvh in range(ATTN_KV_HEADS)]
    scores = []
    for pair in pairs:
        q2 = q_ref[:, lanes[pair]] * (HEAD_DIM ** -0.5)
        s = jax.lax.dot_general(q2, k_pairs[pair // pairs_per_kv], (((1,), (1,)), ((), ())),
                                preferred_element_type=F32)
        scores.append(s + bias_ref[0, pair])
    probs, scales = [], []
    for pair in pairs:
        halves, inv = [], []
        for half in range(2):
            sh = scores[pair][:, half * 2 * ATTN_BLOCK:(half + 1) * 2 * ATTN_BLOCK]
            sink = sink_ref[2 * pair + half]
            m = jnp.maximum(jnp.max(sh, axis=-1, keepdims=True), sink)
            p = jnp.exp(sh - m)
            inv.append(1.0 / (jnp.sum(p, axis=-1, keepdims=True) + jnp.exp(sink - m)))
            halves.append(p.astype(BF16))
        probs.append(jnp.concatenate(halves, axis=1))
        scales.append(jnp.where(low_out, inv[0], inv[1]))
    for pair in pairs:
        o = jnp.dot(probs[pair], v_pairs[pair // pairs_per_kv], preferred_element_type=F32)
        o_ref[:, lanes[pair]] = (o * scales[pair]).astype(o_ref.dtype)


def _attn_bias():
    qi = jnp.arange(ATTN_BLOCK)[:, None]
    sj = jnp.arange(2 * ATTN_BLOCK)[None, :]
    dist = qi + ATTN_BLOCK - sj
    in_window = (dist >= 0) & (dist < ATTN_BLOCK)
    slopes = 2.0 ** (-8.0 * jnp.arange(1, ATTN_HEADS + 1, dtype=F32) / ATTN_HEADS)
    alibi = -slopes[:, None, None] * dist.astype(F32)
    valid = jnp.stack([in_window & (sj >= ATTN_BLOCK), in_window])[:, None]
    bias = jnp.where(valid, alibi[None], NEG_BIG)
    bias = bias.reshape(2, ATTN_HEADS // 2, 2, ATTN_BLOCK, 2 * ATTN_BLOCK).transpose(0, 1, 3, 2, 4)
    return bias.reshape(2, ATTN_HEADS // 2, ATTN_BLOCK, 4 * ATTN_BLOCK)


def _attn(q, kv, sinks, blocks_per_seq):
    rows = q.shape[0]
    bias = _attn_bias()
    per_step = ATTN_BLOCKS_PER_STEP
    steps_per_seq = blocks_per_seq // per_step
    step_rows = per_step * ATTN_BLOCK
    return pl.pallas_call(
        _attn_kernel,
        grid=(rows // step_rows // steps_per_seq, steps_per_seq),
        in_specs=[pl.BlockSpec((step_rows, ATTN_DIM), lambda b, i: (b * steps_per_seq + i, 0)),
                  pl.BlockSpec((step_rows, 4 * KV_DIM), lambda b, i: (b * steps_per_seq + i, 0)),
                  pl.BlockSpec((ATTN_BLOCK, 4 * KV_DIM),
                               lambda b, i: (b * blocks_per_seq + jnp.maximum(i * per_step - 1, 0), 0)),
                  pl.BlockSpec((1,) + bias.shape[1:], lambda b, i: (jnp.minimum(i, 1), 0, 0, 0)),
                  pl.BlockSpec((1,) + bias.shape[1:], lambda b, i: (1, 0, 0, 0)),
                  pl.BlockSpec(memory_space=pltpu.SMEM)],
        out_specs=pl.BlockSpec((step_rows, ATTN_DIM), lambda b, i: (b * steps_per_seq + i, 0)),
        out_shape=jax.ShapeDtypeStruct((rows, ATTN_DIM), BF16),
        compiler_params=_params("arbitrary", "arbitrary"),
        name="swa_attn",
    )(q, kv, kv, bias, bias, sinks)


def _merge_kernel(x_ref, mod_ref, yn_ref, bonus_ref, g_ref, yb_ref, ga_ref, gb_ref, gn_ref, n2_ref,
                  wa_ref, wb_ref, wo_ref, perm_ref, h_ref, u_ref, group_ref):
    bsz, steps, _ = x_ref.shape
    group_rows = bsz * SUBLANES
    to_batch = perm_ref[1]
    for grp in range(steps // SUBLANES):
        rows = yn_ref[grp * SUBLANES:(grp + 1) * SUBLANES].reshape(group_rows, RWKV_DIM)
        group_ref[:, grp] = _permute_rows(to_batch, rows).reshape(bsz, SUBLANES, RWKV_DIM)
    yn = group_ref[...].reshape(ROW_TILE, RWKV_DIM)

    def tile(ref):
        return ref[...].reshape(ROW_TILE, ref.shape[-1])

    ya = ((yn * gn_ref[0:1, :] + gn_ref[1:2, :]) + tile(bonus_ref).astype(F32)) * tile(g_ref).astype(F32)
    pa = _bdot(ya, wa_ref[...])
    pb = jnp.dot(tile(yb_ref), wb_ref[...], preferred_element_type=F32)
    merged = (jax.nn.sigmoid(tile(ga_ref).astype(F32)) * pa
              + jax.nn.sigmoid(tile(gb_ref).astype(F32)) * pb)
    h = tile(x_ref) + _per_batch_rows(mod_ref[2], bsz) * _bdot(merged, wo_ref[...])
    h_ref[...] = h.reshape(h_ref.shape)
    u = _rms_mod(h, n2_ref[...], _per_batch_rows(mod_ref[4], bsz), _per_batch_rows(mod_ref[3], bsz))
    u_ref[...] = u.astype(BF16).reshape(u_ref.shape)


def _merge(x3, mod_rows, yn, bonus, g, yb, ga, gb, gn, norm2_g, wa, wb, wo, perms):
    bsz, seq, _ = x3.shape
    steps = ROW_TILE // bsz

    def tile(width):
        return pl.BlockSpec((bsz, steps, width), lambda i: (0, i, 0))

    return pl.pallas_call(
        _merge_kernel,
        grid=(seq // steps,),
        in_specs=[tile(D_MODEL), _resident(mod_rows.shape),
                  pl.BlockSpec((steps, bsz, RWKV_DIM), lambda i: (i, 0, 0)),
                  tile(RWKV_DIM), tile(RWKV_DIM), tile(ATTN_DIM), tile(D_MODEL), tile(D_MODEL),
                  _resident(gn.shape), _resident((1, D_MODEL)),
                  _resident(wa.shape), _resident(wb.shape), _resident(wo.shape), _resident(perms.shape)],
        out_specs=[tile(D_MODEL), tile(D_MODEL)],
        out_shape=[jax.ShapeDtypeStruct((bsz, seq, D_MODEL), F32),
                   jax.ShapeDtypeStruct((bsz, seq, D_MODEL), BF16)],
        scratch_shapes=[pltpu.VMEM((bsz, steps // SUBLANES, SUBLANES, RWKV_DIM), F32)],
        compiler_params=_params("arbitrary"),
        name="merge",
    )(x3, mod_rows, yn, bonus, g, yb, ga, gb, gn, norm2_g.reshape(1, D_MODEL), wa, wb, wo, perms)


def _ffn_kernel(tiles_per_seq, h_ref, u_ref, mod_ref, wu_ref, cw_ref, cb_ref, wd_ref, fg_ref,
                o_ref, carry_ref, act_ref):
    @pl.when(pl.program_id(0) % tiles_per_seq == 0)
    def _():
        carry_ref[...] = jnp.zeros_like(carry_ref)

    u = u_ref[...]
    row = jax.lax.broadcasted_iota(jnp.int32, (ROW_TILE, FFN_CHUNK), 0)
    chunks = [slice(c * FFN_CHUNK, (c + 1) * FFN_CHUNK) for c in range(D_FF // FFN_CHUNK)]

    def up(cols):
        return (jnp.dot(u, wu_ref[:, cols], preferred_element_type=F32),
                jnp.dot(u, wu_ref[:, D_FF + cols.start:D_FF + cols.stop], preferred_element_type=F32))

    ahead = [up(cols) for cols in chunks[:FFN_LOOKAHEAD]]
    for c, cols in enumerate(chunks):
        gate, val = ahead.pop(0)
        if c + FFN_LOOKAHEAD < len(chunks):
            ahead.append(up(chunks[c + FFN_LOOKAHEAD]))
        back1 = carry_ref[SUBLANES - 1:SUBLANES, cols]
        back2 = carry_ref[SUBLANES - 2:SUBLANES - 1, cols]
        carry_ref[:, cols] = gate[ROW_TILE - SUBLANES:, :]
        shift1 = jnp.where(row == 0, back1, pltpu.roll(gate, 1, 0))
        shift2 = jnp.where(row == 0, back2, jnp.where(row == 1, back1, pltpu.roll(gate, 2, 0)))
        conv = (cw_ref[0:1, cols] * shift2 + cw_ref[1:2, cols] * shift1 + cw_ref[2:3, cols] * gate
                + cb_ref[:, cols])
        act_ref[:, cols] = (conv * jax.nn.sigmoid(conv) * val).astype(BF16)
    acc = jnp.dot(act_ref[...], wd_ref[...], preferred_element_type=F32)
    h = h_ref[...] + mod_ref[0, 5:6, :] * acc
    o_ref[...] = h * jax.lax.rsqrt(jnp.mean(h * h, axis=-1, keepdims=True) + RMS_EPS) * fg_ref[...]


def _ffn(h1, u2, mod3, w_up, conv_w, conv_b, w_down, final_g, tiles_per_seq):
    rows = h1.shape[0]
    return pl.pallas_call(
        functools.partial(_ffn_kernel, tiles_per_seq),
        grid=(rows // ROW_TILE,),
        in_specs=[pl.BlockSpec((ROW_TILE, D_MODEL), lambda i: (i, 0)),
                  pl.BlockSpec((ROW_TILE, D_MODEL), lambda i: (i, 0)),
                  pl.BlockSpec((1, 6, D_MODEL), lambda i: (i // tiles_per_seq, 0, 0)),
                  _resident(w_up.shape), _resident(conv_w.shape),
                  _resident((1, D_FF)), _resident(w_down.shape), _resident((1, D_MODEL))],
        out_specs=pl.BlockSpec((ROW_TILE, D_MODEL), lambda i: (i, 0)),
        out_shape=jax.ShapeDtypeStruct((rows, D_MODEL), F32),
        scratch_shapes=[pltpu.VMEM((SUBLANES, D_FF), F32), pltpu.VMEM((ROW_TILE, D_FF), BF16)],
        compiler_params=_params("arbitrary"),
        name="ffn",
    )(h1, u2, mod3, w_up, conv_w, conv_b.reshape(1, D_FF), w_down, final_g.reshape(1, D_MODEL))


def kernel(x, c, ada_w, ada_b, norm1_g, w_in, rwkv_mu, rwkv_w0, rwkv_w_up, rwkv_a0, rwkv_a_up, rwkv_g_up, rwkv_k_k, rwkv_k_a, rwkv_r_k, rwkv_gn_w, rwkv_gn_b, attn_sinks, w_branch_a, w_branch_b, w_out, norm2_g, ffn_w_up, ffn_conv_w, ffn_conv_b, ffn_w_down, final_g):
    bsz, seq, _ = x.shape
    depth = ada_w.shape[0]
    rows = bsz * seq
    assert depth == 1, "the ffn kernel fuses the final norm, which follows the last layer only"
    assert bsz * RWKV_HEADS == 128 and seq % ROW_TILE == 0 and seq % SCAN_STEPS == 0
    assert ROW_TILE == bsz * SCAN_STEPS, "prep restarts the decay product per tile, the scan per block"
    tiles_per_seq = seq // ROW_TILE
    ones_bd = jnp.kron(jnp.eye(RWKV_HEADS, dtype=F32), jnp.ones((HEAD_DIM, HEAD_DIM), F32)).astype(BF16)
    perms = _row_order_perms(bsz)
    h = x.reshape(rows, D_MODEL)
    for l in range(depth):
        mod3 = _mod(c, ada_w, ada_b[l], l).reshape(bsz, 6, D_MODEL)
        mod_rows = jnp.broadcast_to(mod3.transpose(1, 0, 2)[:, :, None, :], (6, bsz, SUBLANES, D_MODEL))
        kv0 = RWKV_COLS + ATTN_DIM
        kv_cols = kv0 + (jnp.arange(4 * KV_DIM) // (2 * HEAD_DIM)) * HEAD_DIM + jnp.arange(4 * KV_DIM) % HEAD_DIM
        f_rwkv, q, kv, gate_a, gate_b = _in_proj(h, mod3, norm1_g[l], w_in[l].astype(BF16),
                                                 w_in[l][:, kv_cols].astype(BF16), tiles_per_seq)

        lora_w = jnp.zeros((LORA_COLS, 2 * RWKV_DIM), F32)
        lora_w = lora_w.at[:DECAY_LORA, :RWKV_DIM].set(rwkv_w_up[l]).at[DECAY_LORA:, RWKV_DIM:].set(rwkv_a_up[l])
        vecs = jnp.stack([rwkv_w0[l], rwkv_a0[l], rwkv_k_k[l], rwkv_k_a[l], rwkv_r_k[l].reshape(RWKV_DIM)])
        scan_in, pend, g, bonus = _prep(f_rwkv.reshape(bsz, seq, RWKV_COLS), rwkv_mu[l].reshape(1, RWKV_COLS),
                                        lora_w, rwkv_g_up[l], vecs, ones_bd, perms)
        yn = _scan(scan_in.reshape(SCAN_INPUTS, seq, bsz, RWKV_DIM), pend)

        yb = _attn(q, kv, attn_sinks[l], seq // ATTN_BLOCK)

        gn = jnp.stack([rwkv_gn_w[l], rwkv_gn_b[l]])
        by_batch = lambda t: t.reshape(bsz, seq, t.shape[-1])
        h3, u2 = _merge(by_batch(h), mod_rows, yn, bonus, g, by_batch(yb), by_batch(gate_a), by_batch(gate_b),
                        gn, norm2_g[l], w_branch_a[l].astype(BF16), w_branch_b[l].astype(BF16),
                        w_out[l].astype(BF16), perms)
        w_up = ffn_w_up[l].astype(BF16)
        h = _ffn(h3.reshape(rows, D_MODEL), u2.reshape(rows, D_MODEL), mod3, w_up,
                 ffn_conv_w[l], ffn_conv_b[l], ffn_w_down[l].astype(BF16), final_g, tiles_per_seq)
    return h.reshape(bsz, seq, D_MODEL)
```

```python
import functools

import jax
import jax.numpy as jnp
from jax.experimental import pallas as pl
from jax.experimental.pallas import tpu as pltpu

F32 = jnp.float32
BF16 = jnp.bfloat16

D_MODEL = 1024
RWKV_HEADS = 8
HEAD_DIM = 64
RWKV_DIM = RWKV_HEADS * HEAD_DIM
DECAY_LORA = 64
AAA_LORA = 64
GATE_LORA = 128
LORA_COLS = DECAY_LORA + AAA_LORA
RWKV_COLS = 3 * RWKV_DIM + LORA_COLS + GATE_LORA
ATTN_HEADS = 8
ATTN_KV_HEADS = 2
ATTN_GROUP = ATTN_HEADS // ATTN_KV_HEADS
ATTN_DIM = ATTN_HEADS * HEAD_DIM
KV_DIM = ATTN_KV_HEADS * HEAD_DIM
ATTN_BLOCK = 128
ATTN_BLOCKS_PER_STEP = 4
D_FF = 2816
GN_EPS = 64e-5
RMS_EPS = 1e-6
DECAY_SCALE = 0.6065306597126334

VMEM_LIMIT_BYTES = 56 * 1024 * 1024
ROW_TILE = 512
FFN_CHUNK = 256
FFN_LOOKAHEAD = 2
SCAN_STEPS = 32
SCAN_INPUTS = 5
MERGE_INPUT_BUFFERS = 3
UPDATE_GROUPS = 2
SUBLANES = 8
NEG_BIG = -1e30


def _params(*semantics):
    return pltpu.CompilerParams(dimension_semantics=semantics, vmem_limit_bytes=VMEM_LIMIT_BYTES)


def _resident(shape):
    zeros = (0,) * len(shape)
    return pl.BlockSpec(shape, lambda *_: zeros, pipeline_mode=pl.Buffered(1))


def _bdot(a, b):
    return jnp.dot(a.astype(BF16), b.astype(BF16), preferred_element_type=F32)


def _split(x):
    hi = x.astype(BF16)
    lo = (x - hi.astype(F32)).astype(BF16)
    return hi, lo


def _head_sum(x, ones_blockdiag):
    hi, lo = _split(x)
    dot = functools.partial(jnp.dot, preferred_element_type=F32)
    return dot(hi, ones_blockdiag) + dot(lo, ones_blockdiag)


def _rms_mod(h, gain, scale, shift):
    y = h * jax.lax.rsqrt(jnp.mean(h * h, axis=-1, keepdims=True) + RMS_EPS)
    return y * (gain * (1.0 + scale)) + shift


def _permute_rows(p, x, pieces=3):
    dot = functools.partial(jnp.dot, preferred_element_type=F32)
    out = None
    for _ in range(pieces):
        piece = x.astype(BF16)
        x = x - piece.astype(F32)
        out = dot(p, piece) if out is None else out + dot(p, piece)
    return out


def _row_order_perms(bsz):
    rows = bsz * SUBLANES
    src = jnp.arange(rows)[None, :]
    to_time = jnp.arange(rows)[:, None] == (src % SUBLANES) * bsz + src // SUBLANES
    return jnp.stack([to_time, to_time.T]).astype(BF16)


def _per_batch_rows(t, bsz):
    groups = ROW_TILE // bsz // SUBLANES
    return jnp.broadcast_to(t[:, None], (bsz, groups) + t.shape[1:]).reshape(ROW_TILE, t.shape[-1])


def _mod_kernel(c_ref, w_ref, b_ref, o_ref):
    c = c_ref[...]
    o_ref[...] = _bdot(c * jax.nn.sigmoid(c), w_ref[...]) + b_ref[...]


def _mod(c, ada_w, ada_b):
    bsz = c.shape[0]
    cols = ada_w.shape[1]
    tile = cols // 4
    return pl.pallas_call(
        _mod_kernel,
        grid=(cols // tile,),
        in_specs=[pl.BlockSpec((bsz, D_MODEL), lambda j: (0, 0)),
                  pl.BlockSpec((D_MODEL, tile), lambda j: (0, j)),
                  pl.BlockSpec((1, tile), lambda j: (0, j))],
        out_specs=pl.BlockSpec((bsz, tile), lambda j: (0, j)),
        out_shape=jax.ShapeDtypeStruct((bsz, cols), F32),
        compiler_params=_params("arbitrary"),
        name="mod",
    )(c, ada_w, ada_b.reshape(1, cols))


def _in_proj_kernel(x_ref, mod_ref, g_ref, w_ref, wkv_ref, f_ref, q_ref, kv_ref, ga_ref, gb_ref):
    u = _rms_mod(x_ref[...], g_ref[...], mod_ref[0, 1:2, :], mod_ref[0, 0:1, :]).astype(BF16)
    col = 0
    for out in (f_ref, q_ref, None, ga_ref, gb_ref):
        if out is None:
            kv_ref[...] = jnp.dot(u, wkv_ref[...], preferred_element_type=F32).astype(kv_ref.dtype)
            col += 2 * KV_DIM
            continue
        width = out.shape[1]
        out[...] = jnp.dot(u, w_ref[:, col:col + width], preferred_element_type=F32).astype(out.dtype)
        col += width


def _in_proj(x2, mod3, norm_g, w_in, w_kv, tiles_per_seq):
    rows = x2.shape[0]
    widths = (RWKV_COLS, ATTN_DIM, 4 * KV_DIM, D_MODEL, D_MODEL)
    dtypes = (F32, BF16, BF16, BF16, BF16)
    return pl.pallas_call(
        _in_proj_kernel,
        grid=(rows // ROW_TILE,),
        in_specs=[pl.BlockSpec((ROW_TILE, D_MODEL), lambda i: (i, 0)),
                  pl.BlockSpec((1, 6, D_MODEL), lambda i: (i // tiles_per_seq, 0, 0)),
                  _resident((1, D_MODEL)),
                  _resident(w_in.shape), _resident(w_kv.shape)],
        out_specs=[pl.BlockSpec((ROW_TILE, w), lambda i: (i, 0)) for w in widths],
        out_shape=[jax.ShapeDtypeStruct((rows, w), d) for w, d in zip(widths, dtypes)],
        compiler_params=_params("arbitrary"),
        name="in_proj",
    )(x2, mod3, norm_g.reshape(1, D_MODEL), w_in, w_kv)


def _prep_kernel(f_ref, prev_ref, mu_ref, lora_ref, gup_ref, vec_ref, ones_ref, perm_ref,
                 x_ref, pend_ref, g_ref, bonus_ref, group_ref):
    bsz, steps, cols = f_ref.shape
    feat = f_ref[...].reshape(ROW_TILE, cols)
    before = pltpu.roll(prev_ref[...].reshape(bsz * SUBLANES, cols), bsz * SUBLANES - (SUBLANES - 1), 0)
    before = jnp.where(pl.program_id(0) == 0, 0.0, before)
    before = _per_batch_rows(before.reshape(bsz, SUBLANES, cols), bsz)
    row = jax.lax.broadcasted_iota(jnp.int32, feat.shape, 0)
    shifted = jnp.where(row % steps == 0, before, pltpu.roll(feat, 1, 0))
    f = feat + (shifted - feat) * mu_ref[...]

    r = f[:, 0:RWKV_DIM]
    k = f[:, RWKV_DIM:2 * RWKV_DIM]
    v = f[:, 2 * RWKV_DIM:3 * RWKV_DIM]
    lo = f[:, 3 * RWKV_DIM:3 * RWKV_DIM + LORA_COLS]
    g_lo = f[:, 3 * RWKV_DIM + LORA_COLS:]
    w0, a0, k_k, k_a, r_k = (vec_ref[i:i + 1, :] for i in range(5))

    lane = jax.lax.broadcasted_iota(jnp.int32, lo.shape, 1)
    dot = functools.partial(jnp.dot, preferred_element_type=F32)
    lo_hi, lo_lo = _split(jnp.where(lane < DECAY_LORA, jnp.tanh(lo), lo))
    lora_hi, lora_lo = _split(lora_ref[...])
    lora = dot(lo_hi, lora_hi)
    lora_decay = (lora[:, :RWKV_DIM] + dot(lo_lo, lora_hi[:, :RWKV_DIM])) + dot(lo_hi, lora_lo[:, :RWKV_DIM])
    log_decay = -DECAY_SCALE * jax.nn.sigmoid(w0 + lora_decay)
    step_in_tile = jax.lax.broadcasted_iota(jnp.int32, (ROW_TILE, RWKV_DIM), 0) % steps
    cum = log_decay
    shift = 1
    while shift < steps:
        cum = cum + jnp.where(step_in_tile >= shift, pltpu.roll(cum, shift, 0), 0.0)
        shift *= 2
    decayed = jnp.exp(cum)
    grown = jnp.exp(-cum)
    decayed_before = jnp.exp(cum - log_decay)
    a = jax.nn.sigmoid(a0 + lora[:, RWKV_DIM:])
    g = _bdot(jax.nn.sigmoid(g_lo), gup_ref[...])

    ones = ones_ref[...]
    kk = k * k_k
    kk = kk / jnp.maximum(jnp.sqrt(_head_sum(kk * kk, ones)), 1e-12)
    k = k * (1.0 + (a - 1.0) * k_a)

    to_time = perm_ref[0]
    group_rows = bsz * SUBLANES
    groups = steps // SUBLANES

    def time_major(grp, pieces):
        return _permute_rows(to_time, group_ref[:, grp].reshape(group_rows, RWKV_DIM), pieces)

    scan_inputs = (r * decayed, k * grown, v, kk * decayed_before, kk * a * grown)
    for i, t in enumerate(scan_inputs):
        group_ref[...] = t.reshape(bsz, groups, SUBLANES, RWKV_DIM)
        for grp in range(groups):
            x_ref[i, grp * group_rows:(grp + 1) * group_rows, :] = time_major(grp, 2)
    group_ref[...] = decayed.reshape(bsz, groups, SUBLANES, RWKV_DIM)
    pend_ref[...] = time_major(groups - 1, 3)[(SUBLANES - 1) * bsz:]
    g_ref[...] = g.astype(BF16).reshape(bsz, steps, RWKV_DIM)
    bonus_ref[...] = (_bdot(r * k * r_k, ones) * v).astype(BF16).reshape(bsz, steps, RWKV_DIM)


def _prep(f_rwkv, mu, lora_w, g_up, vecs, ones_bd, perms):
    bsz, seq, cols = f_rwkv.shape
    steps = ROW_TILE // bsz
    batch_major = pl.BlockSpec((bsz, steps, RWKV_DIM), lambda i: (0, i, 0))
    return pl.pallas_call(
        _prep_kernel,
        grid=(seq // steps,),
        in_specs=[pl.BlockSpec((bsz, steps, cols), lambda i: (0, i, 0)),
                  pl.BlockSpec((bsz, SUBLANES, cols),
                               lambda i: (0, jnp.maximum(i * (steps // SUBLANES) - 1, 0), 0)),
                  _resident(mu.shape), _resident(lora_w.shape), _resident(g_up.shape),
                  _resident(vecs.shape), _resident(ones_bd.shape), _resident(perms.shape)],
        out_specs=[pl.BlockSpec((SCAN_INPUTS, ROW_TILE, RWKV_DIM), lambda i: (0, i, 0)),
                   pl.BlockSpec((bsz, RWKV_DIM), lambda i: (i, 0)),
                   batch_major, batch_major],
        out_shape=[jax.ShapeDtypeStruct((SCAN_INPUTS, seq * bsz, RWKV_DIM), F32),
                   jax.ShapeDtypeStruct((seq // steps * bsz, RWKV_DIM), F32)]
        + [jax.ShapeDtypeStruct((bsz, seq, RWKV_DIM), BF16)] * 2,
        scratch_shapes=[pltpu.VMEM((bsz, steps // SUBLANES, SUBLANES, RWKV_DIM), F32)],
        compiler_params=_params("arbitrary"),
        name="rwkv_prep",
    )(f_rwkv, f_rwkv, mu, lora_w, g_up, vecs, ones_bd, perms)


def _scan_kernel(x_ref, pend_ref, y_ref, state_ref, rows_even_ref, rows_odd_ref, yraw_ref):
    pairs = state_ref.shape[-1]
    bsz = x_ref.shape[2]
    two_heads = 2 * HEAD_DIM
    head_pairs = [slice(hp * two_heads, (hp + 1) * two_heads) for hp in range(RWKV_HEADS // 2)]
    R, K, V, KK, B = range(SCAN_INPUTS)
    per_group = HEAD_DIM // UPDATE_GROUPS
    in_update = (SCAN_INPUTS - 1) // UPDATE_GROUPS

    def regroup(slab_of):
        xt = jnp.concatenate([slab_of(lanes) for lanes in head_pairs], axis=0).T
        xt = jnp.concatenate([xt[:HEAD_DIM], xt[HEAD_DIM:]], axis=1)
        return xt.reshape(SUBLANES, SUBLANES, pairs)

    def transpose_in(i, t, rows_ref):
        rows_ref[i] = regroup(lambda lanes: x_ref[i, t, :, lanes])

    def write_out(t):
        y = yraw_ref[...]
        cen = y - jnp.mean(y, axis=0, keepdims=True)
        var = jnp.mean(cen * cen, axis=0, keepdims=True)
        yn = cen * jax.lax.rsqrt(var + GN_EPS)
        slab = jnp.concatenate([yn[:, :pairs // 2], yn[:, pairs // 2:]], axis=0).T
        for hp, lanes in enumerate(head_pairs):
            y_ref[t, :, lanes] = slab[hp * bsz:(hp + 1) * bsz]

    @pl.when(pl.program_id(0) == 0)
    def _():
        state_ref[...] = jnp.zeros_like(state_ref)
        yraw_ref[...] = jnp.zeros_like(yraw_ref)

    pend = regroup(lambda lanes: pend_ref[:, lanes]).reshape(HEAD_DIM, pairs)
    for j in range(HEAD_DIM):
        state_ref[j] = state_ref[j] * pend[j:j + 1, :]

    for i in range(SCAN_INPUTS):
        transpose_in(i, 0, rows_even_ref)

    def step(t, rows_ref, next_rows_ref):
        t_next = jnp.minimum(t + 1, SCAN_STEPS - 1)
        write_out(jnp.maximum(t - 1, 0))
        for i in range(in_update * UPDATE_GROUPS, SCAN_INPUTS):
            transpose_in(i, t_next, next_rows_ref)

        acc = jnp.zeros((HEAD_DIM, pairs), F32)
        for j in range(HEAD_DIM):
            acc = acc + state_ref[j] * rows_ref[KK, j // SUBLANES, j % SUBLANES:j % SUBLANES + 1, :]
        sa = -acc
        v = rows_ref[V].reshape(HEAD_DIM, pairs)

        def update(grp, y):
            for u in range(in_update):
                transpose_in(grp * in_update + u, t_next, next_rows_ref)
            for jj in range(per_group):
                j = grp * per_group + jj
                jh = grp * (per_group // SUBLANES) + jj // SUBLANES
                kl = slice(jj % SUBLANES, jj % SUBLANES + 1)
                s_new = state_ref[j] + sa * rows_ref[B, jh, kl, :] + v * rows_ref[K, jh, kl, :]
                state_ref[j] = s_new
                y = y + s_new * rows_ref[R, jh, kl, :]
            return y

        yraw_ref[...] = jax.lax.fori_loop(0, UPDATE_GROUPS, update, jnp.zeros((HEAD_DIM, pairs), F32))

    def two_steps(i, carry):
        step(2 * i, rows_even_ref, rows_odd_ref)
        step(2 * i + 1, rows_odd_ref, rows_even_ref)
        return carry

    jax.lax.fori_loop(0, SCAN_STEPS // 2, two_steps, 0)
    write_out(SCAN_STEPS - 1)


def _scan(x, pend):
    _, seq, bsz, _ = x.shape
    pairs = bsz * RWKV_HEADS
    return pl.pallas_call(
        _scan_kernel,
        grid=(seq // SCAN_STEPS,),
        in_specs=[pl.BlockSpec((SCAN_INPUTS, SCAN_STEPS, bsz, RWKV_DIM), lambda i: (0, i, 0, 0)),
                  pl.BlockSpec((bsz, RWKV_DIM), lambda i: (jnp.maximum(i - 1, 0), 0))],
        out_specs=pl.BlockSpec((SCAN_STEPS, bsz, RWKV_DIM), lambda i: (i, 0, 0)),
        out_shape=jax.ShapeDtypeStruct((seq, bsz, RWKV_DIM), F32),
        scratch_shapes=[pltpu.VMEM((HEAD_DIM, HEAD_DIM, pairs), F32),
                        pltpu.VMEM((SCAN_INPUTS, SUBLANES, SUBLANES, pairs), F32),
                        pltpu.VMEM((SCAN_INPUTS, SUBLANES, SUBLANES, pairs), F32),
                        pltpu.VMEM((HEAD_DIM, pairs), F32)],
        compiler_params=_params("arbitrary"),
        name="rwkv_scan",
    )(x, pend)


def _attn_kernel(q_ref, kv_ref, kvp_ref, bias_first_ref, bias_ref, sink_ref, o_ref):
    kv_all = jnp.concatenate([kvp_ref[...], kv_ref[...]], axis=0)
    for blk in range(ATTN_BLOCKS_PER_STEP):
        rows = slice(blk * ATTN_BLOCK, (blk + 1) * ATTN_BLOCK)
        _attend_block(q_ref.at[rows], kv_all[blk * ATTN_BLOCK:(blk + 2) * ATTN_BLOCK],
                      bias_first_ref if blk == 0 else bias_ref, sink_ref, o_ref.at[rows])


def _attend_block(q_ref, kv, bias_ref, sink_ref, o_ref):
    low = jax.lax.broadcasted_iota(jnp.int32, (2 * ATTN_BLOCK, 2 * HEAD_DIM), 1) < HEAD_DIM
    low_out = jax.lax.broadcasted_iota(jnp.int32, (ATTN_BLOCK, 2 * HEAD_DIM), 1) < HEAD_DIM
    nothing = jnp.zeros((2 * ATTN_BLOCK, 2 * HEAD_DIM), BF16)

    def block_diag(t):
        return jnp.concatenate([jnp.where(low, t, nothing), jnp.where(low, nothing, t)], axis=0)

    pairs = range(ATTN_HEADS // 2)
    pairs_per_kv = ATTN_GROUP // 2
    lanes = [slice(n * 2 * HEAD_DIM, (n + 1) * 2 * HEAD_DIM) for n in range(2 * ATTN_KV_HEADS)]
    k_pairs = [block_diag(kv[:, lanes[kvh]]) for kvh in range(ATTN_KV_HEADS)]
    v_pairs = [block_diag(kv[:, lanes[ATTN_KV_HEADS + kvh]]) for kvh in range(ATTN_KV_HEADS)]
    scores = []
    for pair in pairs:
        q2 = q_ref[:, lanes[pair]] * (HEAD_DIM ** -0.5)
        s = jax.lax.dot_general(q2, k_pairs[pair // pairs_per_kv], (((1,), (1,)), ((), ())),
                                preferred_element_type=F32)
        scores.append(s + bias_ref[0, pair])
    probs, scales = [], []
    for pair in pairs:
        halves, inv = [], []
        for half in range(2):
            sh = scores[pair][:, half * 2 * ATTN_BLOCK:(half + 1) * 2 * ATTN_BLOCK]
            sink = sink_ref[2 * pair + half]
            m = jnp.maximum(jnp.max(sh, axis=-1, keepdims=True), sink)
            p = jnp.exp(sh - m)
            inv.append(1.0 / (jnp.sum(p, axis=-1, keepdims=True) + jnp.exp(sink - m)))
            halves.append(p.astype(BF16))
        probs.append(jnp.concatenate(halves, axis=1))
        scales.append(jnp.where(low_out, inv[0], inv[1]))
    for pair in pairs:
        o = jnp.dot(probs[pair], v_pairs[pair // pairs_per_kv], preferred_element_type=F32)
        o_ref[:, lanes[pair]] = (o * scales[pair]).astype(o_ref.dtype)


def _attn_bias():
    qi = jnp.arange(ATTN_BLOCK)[:, None]
    sj = jnp.arange(2 * ATTN_BLOCK)[None, :]
    dist = qi + ATTN_BLOCK - sj
    in_window = (dist >= 0) & (dist < ATTN_BLOCK)
    slopes = 2.0 ** (-8.0 * jnp.arange(1, ATTN_HEADS + 1, dtype=F32) / ATTN_HEADS)
    alibi = -slopes[:, None, None] * dist.astype(F32)
    valid = jnp.stack([in_window & (sj >= ATTN_BLOCK), in_window])[:, None]
    bias = jnp.where(valid, alibi[None], NEG_BIG)
    bias = bias.reshape(2, ATTN_HEADS // 2, 2, ATTN_BLOCK, 2 * ATTN_BLOCK).transpose(0, 1, 3, 2, 4)
    return bias.reshape(2, ATTN_HEADS // 2, ATTN_BLOCK, 4 * ATTN_BLOCK)


def _attn(q, kv, sinks, blocks_per_seq):
    rows = q.shape[0]
    bias = _attn_bias()
    per_step = ATTN_BLOCKS_PER_STEP
    steps_per_seq = blocks_per_seq // per_step
    step_rows = per_step * ATTN_BLOCK
    return pl.pallas_call(
        _attn_kernel,
        grid=(rows // step_rows // steps_per_seq, steps_per_seq),
        in_specs=[pl.BlockSpec((step_rows, ATTN_DIM), lambda b, i: (b * steps_per_seq + i, 0)),
                  pl.BlockSpec((step_rows, 4 * KV_DIM), lambda b, i: (b * steps_per_seq + i, 0)),
                  pl.BlockSpec((ATTN_BLOCK, 4 * KV_DIM),
                               lambda b, i: (b * blocks_per_seq + jnp.maximum(i * per_step - 1, 0), 0)),
                  pl.BlockSpec((1,) + bias.shape[1:], lambda b, i: (jnp.minimum(i, 1), 0, 0, 0)),
                  pl.BlockSpec((1,) + bias.shape[1:], lambda b, i: (1, 0, 0, 0)),
                  pl.BlockSpec(memory_space=pltpu.SMEM)],
        out_specs=pl.BlockSpec((step_rows, ATTN_DIM), lambda b, i: (b * steps_per_seq + i, 0)),
        out_shape=jax.ShapeDtypeStruct((rows, ATTN_DIM), BF16),
        compiler_params=_params("arbitrary", "arbitrary"),
        name="swa_attn",
    )(q, kv, kv, bias, bias, sinks)


def _merge_kernel(x_ref, mod_ref, yn_ref, bonus_ref, g_ref, yb_ref, ga_ref, gb_ref, gn_ref, n2_ref,
                  wa_ref, wb_ref, wo_ref, perm_ref, h_ref, u_ref, group_ref):
    bsz, steps, _ = x_ref.shape
    group_rows = bsz * SUBLANES
    to_batch = perm_ref[1]
    for grp in range(steps // SUBLANES):
        rows = yn_ref[grp * SUBLANES:(grp + 1) * SUBLANES].reshape(group_rows, RWKV_DIM)
        group_ref[:, grp] = _permute_rows(to_batch, rows).reshape(bsz, SUBLANES, RWKV_DIM)
    yn = group_ref[...].reshape(ROW_TILE, RWKV_DIM)

    def tile(ref):
        return ref[...].reshape(ROW_TILE, ref.shape[-1])

    ya = ((yn * gn_ref[0:1, :] + gn_ref[1:2, :]) + tile(bonus_ref).astype(F32)) * tile(g_ref).astype(F32)
    pa = _bdot(ya, wa_ref[...])
    pb = jnp.dot(tile(yb_ref), wb_ref[...], preferred_element_type=F32)
    merged = (jax.nn.sigmoid(tile(ga_ref).astype(F32)) * pa
              + jax.nn.sigmoid(tile(gb_ref).astype(F32)) * pb)
    h = tile(x_ref) + _per_batch_rows(mod_ref[2], bsz) * _bdot(merged, wo_ref[...])
    h_ref[...] = h.reshape(h_ref.shape)
    u = _rms_mod(h, n2_ref[...], _per_batch_rows(mod_ref[4], bsz), _per_batch_rows(mod_ref[3], bsz))
    u_ref[...] = u.astype(BF16).reshape(u_ref.shape)


def _merge(x3, mod_rows, yn, bonus, g, yb, ga, gb, gn, norm2_g, wa, wb, wo, perms):
    bsz, seq, _ = x3.shape
    steps = ROW_TILE // bsz

    def tile(width, **kw):
        return pl.BlockSpec((bsz, steps, width), lambda i: (0, i, 0), **kw)

    deep = dict(pipeline_mode=pl.Buffered(MERGE_INPUT_BUFFERS))
    row_in_specs = [tile(D_MODEL, **deep),
                    pl.BlockSpec((steps, bsz, RWKV_DIM), lambda i: (i, 0, 0), **deep),
                    tile(RWKV_DIM, **deep), tile(RWKV_DIM, **deep), tile(ATTN_DIM, **deep),
                    tile(D_MODEL, **deep), tile(D_MODEL, **deep)]
    row_out_specs = [tile(D_MODEL), tile(D_MODEL)]

    def outer(x_hbm, yn_hbm, bonus_hbm, g_hbm, yb_hbm, ga_hbm, gb_hbm, mod_ref, gn_ref, n2_ref,
              wa_ref, wb_ref, wo_ref, perm_ref, h_hbm, u_hbm, group_ref):
        def body(x_ref, yn_ref, bonus_ref, g_ref, yb_ref, ga_ref, gb_ref, h_ref, u_ref):
            _merge_kernel(x_ref, mod_ref, yn_ref, bonus_ref, g_ref, yb_ref, ga_ref, gb_ref, gn_ref, n2_ref,
                          wa_ref, wb_ref, wo_ref, perm_ref, h_ref, u_ref, group_ref)

        pltpu.emit_pipeline(body, grid=(seq // steps,), in_specs=row_in_specs, out_specs=row_out_specs)(
            x_hbm, yn_hbm, bonus_hbm, g_hbm, yb_hbm, ga_hbm, gb_hbm, h_hbm, u_hbm)

    in_hbm = pl.BlockSpec(memory_space=pl.ANY)
    in_vmem = pl.BlockSpec(memory_space=pltpu.VMEM)
    return pl.pallas_call(
        outer,
        in_specs=[in_hbm] * 7 + [in_vmem] * 7,
        out_specs=[in_hbm, in_hbm],
        out_shape=[jax.ShapeDtypeStruct((bsz, seq, D_MODEL), F32),
                   jax.ShapeDtypeStruct((bsz, seq, D_MODEL), BF16)],
        scratch_shapes=[pltpu.VMEM((bsz, steps // SUBLANES, SUBLANES, RWKV_DIM), F32)],
        compiler_params=pltpu.CompilerParams(vmem_limit_bytes=VMEM_LIMIT_BYTES),
        name="merge",
    )(x3, yn, bonus, g, yb, ga, gb, mod_rows, gn, norm2_g.reshape(1, D_MODEL), wa, wb, wo, perms)


def _ffn_kernel(tiles_per_seq, h_ref, u_ref, mod_ref, wu_ref, cw_ref, cb_ref, wd_ref, fg_ref,
                o_ref, carry_ref, act_ref):
    @pl.when(pl.program_id(0) % tiles_per_seq == 0)
    def _():
        carry_ref[...] = jnp.zeros_like(carry_ref)

    u = u_ref[...]
    row = jax.lax.broadcasted_iota(jnp.int32, (ROW_TILE, FFN_CHUNK), 0)
    chunks = [slice(c * FFN_CHUNK, (c + 1) * FFN_CHUNK) for c in range(D_FF // FFN_CHUNK)]

    def up(cols):
        return (jnp.dot(u, wu_ref[:, cols], preferred_element_type=F32),
                jnp.dot(u, wu_ref[:, D_FF + cols.start:D_FF + cols.stop], preferred_element_type=F32))

    ahead = [up(cols) for cols in chunks[:FFN_LOOKAHEAD]]
    for c, cols in enumerate(chunks):
        gate, val = ahead.pop(0)
        if c + FFN_LOOKAHEAD < len(chunks):
            ahead.append(up(chunks[c + FFN_LOOKAHEAD]))
        back1 = carry_ref[SUBLANES - 1:SUBLANES, cols]
        back2 = carry_ref[SUBLANES - 2:SUBLANES - 1, cols]
        carry_ref[:, cols] = gate[ROW_TILE - SUBLANES:, :]
        shift1 = jnp.where(row == 0, back1, pltpu.roll(gate, 1, 0))
        shift2 = jnp.where(row == 0, back2, jnp.where(row == 1, back1, pltpu.roll(gate, 2, 0)))
        conv = (cw_ref[0:1, cols] * shift2 + cw_ref[1:2, cols] * shift1 + cw_ref[2:3, cols] * gate
                + cb_ref[:, cols])
        act_ref[:, cols] = (conv * jax.nn.sigmoid(conv) * val).astype(BF16)
    acc = jnp.dot(act_ref[...], wd_ref[...], preferred_element_type=F32)
    h = h_ref[...] + mod_ref[0, 5:6, :] * acc
    o_ref[...] = h * jax.lax.rsqrt(jnp.mean(h * h, axis=-1, keepdims=True) + RMS_EPS) * fg_ref[...]


def _ffn(h1, u2, mod3, w_up, conv_w, conv_b, w_down, final_g, tiles_per_seq):
    rows = h1.shape[0]
    return pl.pallas_call(
        functools.partial(_ffn_kernel, tiles_per_seq),
        grid=(rows // ROW_TILE,),
        in_specs=[pl.BlockSpec((ROW_TILE, D_MODEL), lambda i: (i, 0)),
                  pl.BlockSpec((ROW_TILE, D_MODEL), lambda i: (i, 0)),
                  pl.BlockSpec((1, 6, D_MODEL), lambda i: (i // tiles_per_seq, 0, 0)),
                  _resident(w_up.shape), _resident(conv_w.shape),
                  _resident((1, D_FF)), _resident(w_down.shape), _resident((1, D_MODEL))],
        out_specs=pl.BlockSpec((ROW_TILE, D_MODEL), lambda i: (i, 0)),
        out_shape=jax.ShapeDtypeStruct((rows, D_MODEL), F32),
        scratch_shapes=[pltpu.VMEM((SUBLANES, D_FF), F32), pltpu.VMEM((ROW_TILE, D_FF), BF16)],
        compiler_params=_params("arbitrary"),
        name="ffn",
    )(h1, u2, mod3, w_up, conv_w, conv_b.reshape(1, D_FF), w_down, final_g.reshape(1, D_MODEL))


def kernel(x, c, ada_w, ada_b, norm1_g, w_in, rwkv_mu, rwkv_w0, rwkv_w_up, rwkv_a0, rwkv_a_up, rwkv_g_up, rwkv_k_k, rwkv_k_a, rwkv_r_k, rwkv_gn_w, rwkv_gn_b, attn_sinks, w_branch_a, w_branch_b, w_out, norm2_g, ffn_w_up, ffn_conv_w, ffn_conv_b, ffn_w_down, final_g):
    bsz, seq, _ = x.shape
    depth = ada_w.shape[0]
    rows = bsz * seq
    assert depth == 1, "the ffn kernel fuses the final norm, which follows the last layer only"
    assert bsz * RWKV_HEADS == 128 and seq % ROW_TILE == 0 and seq % SCAN_STEPS == 0
    assert ROW_TILE == bsz * SCAN_STEPS, "prep restarts the decay product per tile, the scan per block"
    tiles_per_seq = seq // ROW_TILE
    ones_bd = jnp.kron(jnp.eye(RWKV_HEADS, dtype=F32), jnp.ones((HEAD_DIM, HEAD_DIM), F32)).astype(BF16)
    perms = _row_order_perms(bsz)
    h = x.reshape(rows, D_MODEL)
    for l in range(depth):
        mod3 = _mod(c, ada_w[l], ada_b[l]).reshape(bsz, 6, D_MODEL)
        mod_rows = jnp.broadcast_to(mod3.transpose(1, 0, 2)[:, :, None, :], (6, bsz, SUBLANES, D_MODEL))
        kv0 = RWKV_COLS + ATTN_DIM
        kv_cols = kv0 + (jnp.arange(4 * KV_DIM) // (2 * HEAD_DIM)) * HEAD_DIM + jnp.arange(4 * KV_DIM) % HEAD_DIM
        f_rwkv, q, kv, gate_a, gate_b = _in_proj(h, mod3, norm1_g[l], w_in[l].astype(BF16),
                                                 w_in[l][:, kv_cols].astype(BF16), tiles_per_seq)

        lora_w = jnp.zeros((LORA_COLS, 2 * RWKV_DIM), F32)
        lora_w = lora_w.at[:DECAY_LORA, :RWKV_DIM].set(rwkv_w_up[l]).at[DECAY_LORA:, RWKV_DIM:].set(rwkv_a_up[l])
        vecs = jnp.stack([rwkv_w0[l], rwkv_a0[l], rwkv_k_k[l], rwkv_k_a[l], rwkv_r_k[l].reshape(RWKV_DIM)])
        scan_in, pend, g, bonus = _prep(f_rwkv.reshape(bsz, seq, RWKV_COLS), rwkv_mu[l].reshape(1, RWKV_COLS),
                                        lora_w, rwkv_g_up[l], vecs, ones_bd, perms)
        yn = _scan(scan_in.reshape(SCAN_INPUTS, seq, bsz, RWKV_DIM), pend)

        yb = _attn(q, kv, attn_sinks[l], seq // ATTN_BLOCK)

        gn = jnp.stack([rwkv_gn_w[l], rwkv_gn_b[l]])
        by_batch = lambda t: t.reshape(bsz, seq, t.shape[-1])
        h3, u2 = _merge(by_batch(h), mod_rows, yn, bonus, g, by_batch(yb), by_batch(gate_a), by_batch(gate_b),
                        gn, norm2_g[l], w_branch_a[l].astype(BF16), w_branch_b[l].astype(BF16),
                        w_out[l].astype(BF16), perms)
        w_up = ffn_w_up[l].astype(BF16)
        h = _ffn(h3.reshape(rows, D_MODEL), u2.reshape(rows, D_MODEL), mod3, w_up,
                 ffn_conv_w[l], ffn_conv_b[l], ffn_w_down[l].astype(BF16), final_g, tiles_per_seq)
    return h.reshape(bsz, seq, D_MODEL)
```
